```python
import math
import jax, jax.numpy as jnp
from jax import lax
import numpy as np

D_MODEL = 4096
BATCH = 32
SEQ = 256
DEPTH = 4
DEC_BATCH = 4
DEC_SEQ = 1024
PAST_LEN = 512

GRID_W = 64
HEAD_DIM = 128
MIX_W = D_MODEL
GROUP_W = MIX_W // 4
A_HEADS = GROUP_W // (2 * HEAD_DIM)
B_W = GROUP_W
C_HEADS = 8
C_DK = 128
C_DV = GROUP_W // C_HEADS
C_W = C_HEADS * C_DK
D_HEADS = GROUP_W // HEAD_DIM
D_KV_HEADS = 2
D_GROUPS = D_HEADS // D_KV_HEADS
D_FF = 4 * D_MODEL
Q_BLOCK = 128
CHUNK = 32
ROPE_BASE = 10000.0
ROPE_FREQS = HEAD_DIM // 4
HY_NFREQ = 16
HY_EMB = 1 + 2 * HY_NFREQ
HY_FFN = 64
HY_SHIFT = 0.05
HY_DECAY_MIN = 3.07
HY_DECAY_MAX = 15.35
EPS = 1e-6
GATE_FLOOR = 1e-20
IN_SPLITS = (A_HEADS * 2 * HEAD_DIM, A_HEADS * 2 * HEAD_DIM, A_HEADS * 2 * HEAD_DIM, 3 * B_W, C_W, C_HEADS * C_DV, C_W, C_W, C_HEADS * C_DV, D_HEADS * HEAD_DIM, D_KV_HEADS * HEAD_DIM, D_KV_HEADS * HEAD_DIM)
IN_W = sum(IN_SPLITS)

kernel_name = 'hybrid_diffusion_parallel_groups_step'


def _rms(x, g):
    xf = x.astype(jnp.float32)
    y = xf * lax.rsqrt(jnp.mean(xf * xf, axis=-1, keepdims=True) + EPS)
    return (y * g.astype(jnp.float32)).astype(x.dtype)


def _modulation(cvec, w_mod, b_mod):
    m = jax.nn.silu(cvec) @ w_mod + b_mod
    return jnp.split(m, 6, axis=-1)


def _in_proj(h, w_in):
    return jnp.split(h @ w_in, [int(s) for s in np.cumsum(IN_SPLITS)[:-1]], axis=-1)


def _rope_half(x, ang):
    x1, x2 = x[..., :ROPE_FREQS], x[..., ROPE_FREQS:]
    cos, sin = jnp.cos(ang), jnp.sin(ang)
    return jnp.concatenate([x1 * cos - x2 * sin, x1 * sin + x2 * cos], axis=-1)


def _axial_rope(x):
    L = x.shape[1]
    n_rows = L // GRID_W
    rows = jnp.repeat(jnp.arange(n_rows, dtype=jnp.float32), GRID_W)
    cols = jnp.tile(jnp.arange(GRID_W, dtype=jnp.float32), n_rows)
    inv = ROPE_BASE ** (-jnp.arange(ROPE_FREQS, dtype=jnp.float32) / ROPE_FREQS)
    shape = (1, L) + (1,) * (x.ndim - 3) + (ROPE_FREQS,)
    ang_r = (rows[:, None] * inv).reshape(shape)
    ang_c = (cols[:, None] * inv).reshape(shape)
    half = HEAD_DIM // 2
    xf = x.astype(jnp.float32)
    out = jnp.concatenate([_rope_half(xf[..., :half], ang_r), _rope_half(xf[..., half:], ang_c)], axis=-1)
    return out.astype(x.dtype)


def _map_query_blocks(f, q):
    B, L = q.shape[:2]
    nb = L // Q_BLOCK
    qb = jnp.moveaxis(q.reshape((B, nb, Q_BLOCK) + q.shape[2:]), 1, 0)
    out = lax.map(f, qb)
    return jnp.moveaxis(out, 0, 1).reshape((B, L) + out.shape[3:])


def _diff_lambda(lam_p, lam_init):
    lp = lam_p.astype(jnp.float32)
    return jnp.exp(jnp.sum(lp[0] * lp[1])) - jnp.exp(jnp.sum(lp[2] * lp[3])) + lam_init


def _diff_attn(q, k, v, lam):
    scale = HEAD_DIM ** -0.5
    def blk(qb):
        s = jnp.einsum('bqhcd,bkhcd->bchqk', qb, k).astype(jnp.float32) * scale
        p = jax.nn.softmax(s, axis=-1)
        a = p[:, 0] - lam * p[:, 1]
        return jnp.einsum('bhqk,bkhe->bqhe', a.astype(v.dtype), v)
    return _map_query_blocks(blk, q)


def _diff_out(o, subln, lam_init):
    B, L = o.shape[:2]
    return (_rms(o, subln) * (1.0 - lam_init)).reshape(B, L, A_HEADS * 2 * HEAD_DIM)


def _gqa_inputs(qd, kd, vd, qn, kn):
    B, L = qd.shape[:2]
    q = _rms(qd.reshape(B, L, D_KV_HEADS, D_GROUPS, HEAD_DIM), qn)
    k = _rms(kd.reshape(B, L, D_KV_HEADS, HEAD_DIM), kn)
    return q, k, vd.reshape(B, L, D_KV_HEADS, HEAD_DIM)


def _gqa(q, k, v):
    scale = HEAD_DIM ** -0.5
    def blk(qb):
        s = jnp.einsum('bqngd,bknd->bngqk', qb, k).astype(jnp.float32) * scale
        p = jax.nn.softmax(s, axis=-1)
        return jnp.einsum('bngqk,bknd->bqngd', p.astype(v.dtype), v)
    return _map_query_blocks(blk, q)


def _short_conv(u, w, b):
    up = jnp.pad(u, ((0, 0), (1, 1), (0, 0)))
    return up[:, :-2] * w[0] + up[:, 1:-1] * w[1] + up[:, 2:] * w[2] + b


def _hyena_filters(L, w1, b1, w2, b2, w3, freq, delta):
    t = jnp.arange(L, dtype=jnp.float32) / L
    bands = jnp.arange(1, HY_NFREQ + 1, dtype=jnp.float32)
    ang = 2.0 * math.pi * t[:, None] * bands
    z = jnp.concatenate([t[:, None], jnp.cos(ang), jnp.sin(ang)], axis=-1)
    hid = jnp.sin(freq * (z @ w1 + b1))
    hid = jnp.sin(freq * (hid @ w2 + b2))
    h = (hid @ w3).reshape(L, 2, 2, B_W)
    h = h * (jnp.exp(-t[:, None, None, None] * jnp.abs(delta)) + HY_SHIFT)
    return h / (jnp.sum(jnp.abs(h), axis=0, keepdims=True) + EPS)


def _causal_fft_conv(u, h):
    L = u.shape[1]
    n = 2 * L
    y = jnp.fft.irfft(jnp.fft.rfft(u, n=n, axis=1) * jnp.fft.rfft(h, n=n, axis=0)[None], n=n, axis=1)
    return y[:, :L]


def _bidir_long_conv(u, h_fwd, h_bwd, skip):
    fwd = _causal_fft_conv(u, h_fwd)
    bwd = jnp.flip(_causal_fft_conv(jnp.flip(u, axis=1), h_bwd), axis=1)
    return fwd + bwd + u * skip


def _hyena(u, lp):
    L = u.shape[1]
    u = _short_conv(u, lp['b_conv_w'], lp['b_conv_b']).astype(jnp.float32)
    v, x1, x2 = jnp.split(u, 3, axis=-1)
    h = _hyena_filters(L, lp['b_ffn_w1'], lp['b_ffn_b1'], lp['b_ffn_w2'], lp['b_ffn_b2'], lp['b_ffn_w3'], lp['b_freq'], lp['b_delta'])
    skip = lp['b_skip'].astype(jnp.float32)
    z = x1 * _bidir_long_conv(v, h[:, 0, 0], h[:, 1, 0], skip[0])
    return x2 * _bidir_long_conv(z, h[:, 0, 1], h[:, 1, 1], skip[1])


def _hgrn_gate(fz, lb):
    B, L = fz.shape[:2]
    z = fz.astype(jnp.float32)
    f = lb + (1.0 - lb) * jax.nn.sigmoid(z)
    logf = jnp.log(jnp.maximum(f, GATE_FLOOR))
    k = (1.0 - lb) * jax.nn.sigmoid(-z)
    return logf.reshape(B, L, C_HEADS, C_DK), k.reshape(B, L, C_HEADS, C_DK)


def _chunk_scan(q, k, v, logf, S0):
    B, L, H, DK = q.shape
    nc = L // CHUNK
    def chunks(a):
        return jnp.moveaxis(a.reshape(B, nc, CHUNK, H, a.shape[-1]), 1, 0)
    mask = jnp.tril(jnp.ones((CHUNK, CHUNK), bool))[None, :, :, None, None]
    def step(S, inp):
        qc, kc, vc, gc = inp
        G = jnp.cumsum(gc, axis=1)
        diff = jnp.minimum(G[:, :, None] - G[:, None, :], 0.0)
        decay = jnp.where(mask, jnp.exp(diff), 0.0)
        scores = jnp.einsum('bthk,bshk,btshk->bhts', qc, kc, decay)
        o = jnp.einsum('bhts,bshv->bthv', scores, vc) + jnp.einsum('bthk,bhkv->bthv', qc * jnp.exp(G), S)
        G_end = G[:, -1]
        S = jnp.exp(G_end)[..., None] * S + jnp.einsum('bshk,bshv->bhkv', kc * jnp.exp(G_end[:, None] - G), vc)
        return S, o
    S, o = lax.scan(step, S0.astype(jnp.float32), (chunks(q), chunks(k), chunks(v), chunks(logf)))
    return jnp.moveaxis(o, 0, 1).reshape(B, L, H, v.shape[-1]), S


def _hgrn2(qc, ic, ffz, fbz, gz, lb, gnorm, S0):
    B, L = qc.shape[:2]
    q = jax.nn.silu(qc.astype(jnp.float32)).reshape(B, L, C_HEADS, C_DK)
    v = ic.astype(jnp.float32).reshape(B, L, C_HEADS, C_DV)
    logf_f, k_f = _hgrn_gate(ffz, lb[0])
    logf_b, k_b = _hgrn_gate(fbz, lb[1])
    o_f, s_f = _chunk_scan(q, k_f, v, logf_f, S0[:, 0])
    o_b, s_b = _chunk_scan(jnp.flip(q, 1), jnp.flip(k_b, 1), jnp.flip(v, 1), jnp.flip(logf_b, 1), S0[:, 1])
    o = _rms(o_f + jnp.flip(o_b, 1), gnorm) * jax.nn.silu(gz.astype(jnp.float32)).reshape(B, L, C_HEADS, C_DV)
    return o.reshape(B, L, C_HEADS * C_DV), jnp.stack([s_f, s_b], axis=1)


def _finish(x, outs, gt1, sh2, sc2, gt2, lp):
    B, L = x.shape[:2]
    mix = jnp.concatenate([o.reshape(B, L, -1).astype(x.dtype) for o in outs], axis=-1) @ lp['w_out']
    x = x + gt1 * mix
    h = _rms(x, lp['g_norm2']) * (1 + sc2) + sh2
    return x + gt2 * (jnp.square(jax.nn.relu(h @ lp['w_mlp1'])) @ lp['w_mlp2'])


def _context_layer(x, lp, lam_init):
    B, L = x.shape[:2]
    sh1, sc1, gt1, sh2, sc2, gt2 = _modulation(lp['c_ctx'], lp['w_mod'], lp['b_mod'])
    h = _rms(x, lp['g_norm1']) * (1 + sc1) + sh1
    qa, ka, va, ub, qc, ic, ffz, fbz, gz, qd, kd, vd = _in_proj(h, lp['w_in'])
    lam = _diff_lambda(lp['a_lam'], lam_init)
    ka = ka.reshape(B, L, A_HEADS, 2 * HEAD_DIM)
    va = va.reshape(B, L, A_HEADS, 2 * HEAD_DIM)
    oa = _diff_out(_diff_attn(qa.reshape(B, L, A_HEADS, 2, HEAD_DIM), ka.reshape(B, L, A_HEADS, 2, HEAD_DIM), va, lam), lp['a_subln'], lam_init)
    ob = _hyena(ub, lp)
    S0 = jnp.zeros((B, 2, C_HEADS, C_DK, C_DV), jnp.float32)
    oc, s_ctx = _hgrn2(qc, ic, ffz, fbz, gz, lp['c_lb'], lp['c_gnorm'], S0)
    qd, kd, vd = _gqa_inputs(qd, kd, vd, lp['d_qnorm'], lp['d_knorm'])
    od = _gqa(qd, kd, vd)
    x = _finish(x, (oa, ob, oc, od), gt1, sh2, sc2, gt2, lp)
    return x, ka, va, kd, vd, s_ctx


def _latent_layer(x, c, lp, lam_init, ctx_ak, ctx_av, ctx_dk, ctx_dv, ctx_s):
    B, L = x.shape[:2]
    P = ctx_ak.shape[1]
    sh1, sc1, gt1, sh2, sc2, gt2 = [m[:, None, :] for m in _modulation(c, lp['w_mod'], lp['b_mod'])]
    h = _rms(x, lp['g_norm1']) * (1 + sc1) + sh1
    qa, ka, va, ub, qc, ic, ffz, fbz, gz, qd, kd, vd = _in_proj(h, lp['w_in'])
    lam = _diff_lambda(lp['a_lam'], lam_init)
    qa = _axial_rope(qa.reshape(B, L, A_HEADS, 2, HEAD_DIM))
    ka = jnp.concatenate([_axial_rope(ka.reshape(B, L, A_HEADS, 2, HEAD_DIM)), ctx_ak.reshape(B, P, A_HEADS, 2, HEAD_DIM)], axis=1)
    va = jnp.concatenate([va.reshape(B, L, A_HEADS, 2 * HEAD_DIM), ctx_av], axis=1)
    oa = _diff_out(_diff_attn(qa, ka, va, lam), lp['a_subln'], lam_init)
    ob = _hyena(ub, lp)
    oc, _ = _hgrn2(qc, ic, ffz, fbz, gz, lp['c_lb'], lp['c_gnorm'], ctx_s)
    qd, kd, vd = _gqa_inputs(qd, kd, vd, lp['d_qnorm'], lp['d_knorm'])
    od = _gqa(_axial_rope(qd), jnp.concatenate([_axial_rope(kd), ctx_dk], axis=1), jnp.concatenate([vd, ctx_dv], axis=1))
    return _finish(x, (oa, ob, oc, od), gt1, sh2, sc2, gt2, lp)


def setup_inputs(seed: int = 0) -> dict:
    key = jax.random.key(seed)
    keys = list(jax.random.split(key, 40))
    def nrm(shape, scale):
        return scale * jax.random.normal(keys.pop(), shape, jnp.float32)
    D = D_MODEL
    return {
        'x_prompt': nrm((BATCH, SEQ, D), 1.0),
        'x_sample': nrm((DEC_BATCH, DEC_SEQ, D), 1.0),
        'c': nrm((DEC_BATCH, D), 1.0),
        'cache_a_k': nrm((DEC_BATCH, DEPTH, PAST_LEN, A_HEADS, 2 * HEAD_DIM), 1.0),
        'cache_a_v': nrm((DEC_BATCH, DEPTH, PAST_LEN, A_HEADS, 2 * HEAD_DIM), 1.0),
        'cache_d_k': nrm((DEC_BATCH, DEPTH, PAST_LEN, D_KV_HEADS, HEAD_DIM), 1.0),
        'cache_d_v': nrm((DEC_BATCH, DEPTH, PAST_LEN, D_KV_HEADS, HEAD_DIM), 1.0),
        'state_c': nrm((DEC_BATCH, DEPTH, 2, C_HEADS, C_DK, C_DV), 0.3),
        'c_ctx': nrm((D,), 1.0),
        'w_mod': nrm((DEPTH, D, 6 * D), 0.5 * D ** -0.5),
        'b_mod': nrm((DEPTH, 6 * D), 0.02),
        'g_norm1': 1.0 + nrm((DEPTH, D), 0.02),
        'g_norm2': 1.0 + nrm((DEPTH, D), 0.02),
        'w_in': nrm((DEPTH, D, IN_W), D ** -0.5),
        'w_out': nrm((DEPTH, MIX_W, D), MIX_W ** -0.5),
        'a_lam': nrm((DEPTH, 4, HEAD_DIM), 0.1),
        'a_subln': 1.0 + nrm((DEPTH, 2 * HEAD_DIM), 0.02),
        'b_conv_w': nrm((DEPTH, 3, 3 * B_W), 0.5),
        'b_conv_b': nrm((DEPTH, 3 * B_W), 0.02),
        'b_ffn_w1': nrm((DEPTH, HY_EMB, HY_FFN), HY_EMB ** -0.5),
        'b_ffn_b1': nrm((DEPTH, HY_FFN), 0.1),
        'b_ffn_w2': nrm((DEPTH, HY_FFN, HY_FFN), HY_FFN ** -0.5),
        'b_ffn_b2': nrm((DEPTH, HY_FFN), 0.1),
        'b_ffn_w3': nrm((DEPTH, HY_FFN, 4 * B_W), HY_FFN ** -0.5),
        'b_freq': 1.0 + nrm((DEPTH, HY_FFN), 0.02),
        'b_delta': jnp.linspace(HY_DECAY_MIN, HY_DECAY_MAX, B_W, dtype=jnp.float32) + nrm((DEPTH, 2, 2, B_W), 0.1),
        'b_skip': nrm((DEPTH, 2, B_W), 0.5),
        'c_lb_raw': nrm((DEPTH, 2, C_W), 0.1),
        'c_gnorm': 1.0 + nrm((DEPTH, C_DV), 0.02),
        'd_qnorm': 1.0 + nrm((DEPTH, HEAD_DIM), 0.02),
        'd_knorm': 1.0 + nrm((DEPTH, HEAD_DIM), 0.02),
        'w_mlp1': nrm((DEPTH, D, D_FF), D ** -0.5),
        'w_mlp2': nrm((DEPTH, D_FF, D), D_FF ** -0.5),
        'g_final': 1.0 + nrm((D,), 0.02),
    }


def reference(x_prompt, x_sample, c, cache_a_k, cache_a_v, cache_d_k, cache_d_v, state_c, c_ctx, w_mod, b_mod, g_norm1, g_norm2, w_in, w_out, a_lam, a_subln, b_conv_w, b_conv_b, b_ffn_w1, b_ffn_b1, b_ffn_w2, b_ffn_b2, b_ffn_w3, b_freq, b_delta, b_skip, c_lb_raw, c_gnorm, d_qnorm, d_knorm, w_mlp1, w_mlp2, g_final):
    p = jax.nn.softmax(c_lb_raw.astype(jnp.float32), axis=0)
    lb_all = jnp.cumsum(p, axis=0) - p[:1]
    xp, xs = x_prompt, x_sample
    list_ak, list_av, list_dk, list_dv, list_s = [], [], [], [], []
    for l in range(DEPTH):
        lam_init = 0.8 - 0.6 * math.exp(-0.3 * l)
        lp = dict(c_ctx=c_ctx, w_mod=w_mod[l], b_mod=b_mod[l], g_norm1=g_norm1[l], g_norm2=g_norm2[l],
                  w_in=w_in[l], w_out=w_out[l], a_lam=a_lam[l], a_subln=a_subln[l],
                  b_conv_w=b_conv_w[l], b_conv_b=b_conv_b[l], b_ffn_w1=b_ffn_w1[l], b_ffn_b1=b_ffn_b1[l],
                  b_ffn_w2=b_ffn_w2[l], b_ffn_b2=b_ffn_b2[l], b_ffn_w3=b_ffn_w3[l], b_freq=b_freq[l],
                  b_delta=b_delta[l], b_skip=b_skip[l], c_lb=lb_all[l], c_gnorm=c_gnorm[l],
                  d_qnorm=d_qnorm[l], d_knorm=d_knorm[l], w_mlp1=w_mlp1[l], w_mlp2=w_mlp2[l])
        xp, ak, av, dk, dv, s = _context_layer(xp, lp, lam_init)
        list_ak.append(ak)
        list_av.append(av)
        list_dk.append(dk)
        list_dv.append(dv)
        list_s.append(s)
        xs = _latent_layer(xs, c, lp, lam_init, cache_a_k[:, l], cache_a_v[:, l], cache_d_k[:, l], cache_d_v[:, l], state_c[:, l])
    y_prompt = _rms(xp, g_final)
    y_sample = _rms(xs, g_final)
    new_a_k = jnp.stack(list_ak, axis=1)
    new_a_v = jnp.stack(list_av, axis=1)
    new_d_k = jnp.stack(list_dk, axis=1)
    new_d_v = jnp.stack(list_dv, axis=1)
    new_state_c = jnp.stack(list_s, axis=1)
    return (y_prompt, y_sample, new_a_k, new_a_v, new_d_k, new_d_v, new_state_c)
```

```python
import functools
import math

import jax
import jax.numpy as jnp
import numpy as np
from jax import lax
from jax.experimental import pallas as pl
from jax.experimental.pallas import tpu as pltpu

D_MODEL = 4096
BATCH = 32
SEQ = 256
DEPTH = 4
DEC_BATCH = 4
DEC_SEQ = 1024
PAST_LEN = 512
GRID_W = 64
HEAD_DIM = 128
GROUP_W = D_MODEL // 4
A_HEADS = GROUP_W // (2 * HEAD_DIM)
B_W = GROUP_W
C_HEADS = 8
C_DK = 128
C_DV = GROUP_W // C_HEADS
C_W = C_HEADS * C_DK
D_HEADS = GROUP_W // HEAD_DIM
D_KV_HEADS = 2
D_GROUPS = D_HEADS // D_KV_HEADS
D_FF = 4 * D_MODEL
ROPE_BASE = 10000.0
ROPE_FREQS = HEAD_DIM // 4
HY_NFREQ = 16
HY_EMB = 1 + 2 * HY_NFREQ
HY_FFN = 64
HY_SHIFT = 0.05
EPS = 1e-6
GATE_FLOOR = 1e-20

N_CTX = BATCH * SEQ
N_LAT = DEC_BATCH * DEC_SEQ
N_TOK = N_CTX + N_LAT

OFF_QA = 0
OFF_KA = OFF_QA + GROUP_W
OFF_VA = OFF_KA + GROUP_W
OFF_UB = OFF_VA + GROUP_W
OFF_QC = OFF_UB + 3 * B_W
OFF_IC = OFF_QC + C_W
OFF_FF = OFF_IC + C_HEADS * C_DV
OFF_FB = OFF_FF + C_W
OFF_GZ = OFF_FB + C_W
OFF_QD = OFF_GZ + C_HEADS * C_DV
OFF_KD = OFF_QD + D_HEADS * HEAD_DIM
OFF_VD = OFF_KD + D_KV_HEADS * HEAD_DIM
IN_W = OFF_VD + D_KV_HEADS * HEAD_DIM

LANES = 128
SUBLANES = 8
VMEM_LIMIT = 56 * 1024 * 1024

CHUNK = 128

_HI = lax.Precision.HIGHEST
_BF = jnp.bfloat16
_F32 = jnp.float32


def _cparams(sem):
    return pltpu.CompilerParams(dimension_semantics=sem, vmem_limit_bytes=VMEM_LIMIT)


def _dot_nt(a, b):
    return lax.dot_general(a, b, (((1,), (1,)), ((), ())), preferred_element_type=_F32)


def _dot_tn(a, b):
    return lax.dot_general(a, b, (((0,), (0,)), ((), ())), preferred_element_type=_F32)


def _dot(a, b, precision=None):
    return jnp.dot(a, b, preferred_element_type=_F32, precision=precision)


def _hgrn_chunk(qc, kc, vc, lf, st, reverse):
    C = CHUNK
    row = lax.broadcasted_iota(jnp.int32, (C, C), 0)
    col = lax.broadcasted_iota(jnp.int32, (C, C), 1)
    tri = jnp.where((col >= row) if reverse else (col <= row), 1.0, 0.0).astype(_F32)
    G = _dot(tri, lf, precision=_HI)

    nb = C // SUBLANES
    G3 = G.reshape(nb, SUBLANES, LANES)
    q3 = qc.reshape(nb, SUBLANES, LANES)
    k3 = kc.reshape(nb, SUBLANES, LANES)
    v3 = vc.reshape(nb, SUBLANES, LANES)
    sub = lax.broadcasted_iota(jnp.int32, (nb, SUBLANES, LANES), 1)
    ones = jnp.ones((LANES, LANES), _BF)
    o = jnp.zeros((C, LANES), _F32)
    for s in range(SUBLANES):
        Gs = jnp.broadcast_to(G3[:, s:s + 1, :], G3.shape)
        ks = jnp.broadcast_to(k3[:, s:s + 1, :], G3.shape)
        vs = jnp.broadcast_to(v3[:, s:s + 1, :], G3.shape)
        dec = jnp.exp(jnp.minimum(G3 - Gs, 0.0))
        p = (q3 * ks * dec).reshape(C, LANES)
        r = _dot(p.astype(_BF), ones).reshape(nb, SUBLANES, LANES)
        keep = (sub <= s) if reverse else (sub >= s)
        o = o + (jnp.where(keep, r, 0.0) * vs).reshape(C, LANES)

    scores = jnp.zeros((C, C), _F32)
    rowv = lax.broadcasted_iota(jnp.int32, (C, LANES), 0)
    blk = 2 * SUBLANES
    while blk <= C:
        half = blk // 2
        GB = G.reshape(C // blk, blk, LANES)
        rr = half if reverse else half - 1
        Gref = jnp.broadcast_to(GB[:, rr:rr + 1, :], GB.shape).reshape(C, LANES)
        upper = (rowv % blk) >= half
        qside = jnp.logical_not(upper) if reverse else upper
        e = jnp.exp(jnp.where(qside, G - Gref, Gref - G))
        qt = jnp.where(qside, qc * e, 0.0).astype(_BF)
        kt = jnp.where(qside, 0.0, kc * e).astype(_BF)
        sc = _dot_nt(qt, kt)
        scores = scores + jnp.where((row // blk) == (col // blk), sc, 0.0)
        blk *= 2
    o = o + _dot(scores.astype(_BF), vc.astype(_BF))

    o = o + _dot_nt((qc * jnp.exp(G)).astype(_BF), st.astype(_BF))
    gend = G[0:1, :] if reverse else G[C - 1:C, :]
    kdec = (kc * jnp.exp(gend - G)).astype(_BF)
    st_new = st * jnp.exp(gend) + _dot_tn(vc.astype(_BF), kdec)
    return o, st_new


def _hgrn2_kernel(*refs, L, has_state, emit_state):
    it = iter(refs)
    q_ref, v_ref, zf_ref, zb_ref, gz_ref, lb_ref, gn_ref = (next(it) for _ in range(7))
    s0_ref = next(it) if has_state else None
    o_ref = next(it)
    sout_ref = next(it) if emit_state else None
    of_ref, st_ref = next(it), next(it)

    nchunk = L // CHUNK
    for d in range(2):
        z_ref = zf_ref if d == 0 else zb_ref
        lb = lb_ref[d:d + 1, :]
        if has_state:
            st_ref[...] = s0_ref[d].T
        else:
            st_ref[...] = jnp.zeros((C_DV, C_DK), _F32)

        def body(i, carry, z_ref=z_ref, lb=lb, d=d):
            c = (nchunk - 1 - i) if d == 1 else i
            rows = pl.ds(pl.multiple_of(c * CHUNK, CHUNK), CHUNK)
            z = z_ref[rows, :]
            f = lb + (1.0 - lb) * jax.nn.sigmoid(z)
            lf = jnp.log(jnp.maximum(f, GATE_FLOOR))
            kk = (1.0 - lb) * jax.nn.sigmoid(-z)
            qq = q_ref[rows, :]
            qq = qq * jax.nn.sigmoid(qq)
            o, st_new = _hgrn_chunk(qq, kk, v_ref[rows, :], lf, st_ref[...], reverse=(d == 1))
            st_ref[...] = st_new
            if d == 0:
                of_ref[rows, :] = o
            else:
                tot = of_ref[rows, :] + o
                y = tot * lax.rsqrt(jnp.mean(tot * tot, axis=-1, keepdims=True) + EPS) * gn_ref[...]
                g = gz_ref[rows, :]
                o_ref[rows, :] = (y * (g * jax.nn.sigmoid(g))).astype(o_ref.dtype)
            return carry

        lax.fori_loop(0, nchunk, body, 0)
        if emit_state:
            sout_ref[d] = st_ref[...].T


def _hgrn2(proj, lb, gnorm, s0, *, row0, nb, L, layer):
    has_state = s0 is not None
    emit_state = not has_state
    rb0 = row0 // L

    def col(off):
        return pl.BlockSpec((L, LANES), lambda b, h, off=off: (rb0 + b, off // LANES + h))

    in_specs = [col(OFF_QC), col(OFF_IC), col(OFF_FF), col(OFF_FB), col(OFF_GZ),
                pl.BlockSpec((2, LANES), lambda b, h: (0, h)),
                pl.BlockSpec((1, C_DV), lambda b, h: (0, 0))]
    args = [proj, proj, proj, proj, proj, lb, gnorm]
    if has_state:
        in_specs.append(pl.BlockSpec((None, None, 2, None, C_DK, C_DV), lambda b, h: (b, layer, 0, h, 0, 0)))
        args.append(s0)
    out_shape = [jax.ShapeDtypeStruct((nb * L, C_W), _BF)]
    out_specs = [pl.BlockSpec((L, LANES), lambda b, h: (b, h))]
    if emit_state:
        out_shape.append(jax.ShapeDtypeStruct((nb, 2, C_HEADS, C_DK, C_DV), _F32))
        out_specs.append(pl.BlockSpec((None, 2, None, C_DK, C_DV), lambda b, h: (b, 0, h, 0, 0)))
    res = pl.pallas_call(
        functools.partial(_hgrn2_kernel, L=L, has_state=has_state, emit_state=emit_state),
        grid=(nb, C_HEADS),
        in_specs=in_specs,
        out_specs=out_specs,
        out_shape=out_shape,
        scratch_shapes=[pltpu.VMEM((L, C_DV), _F32), pltpu.VMEM((C_DV, C_DK), _F32)],
        compiler_params=_cparams(("parallel", "parallel")),
        name="hgrn2_lat" if has_state else "hgrn2_ctx",
    )(*args)
    return (res[0], res[1]) if emit_state else (res[0], None)


MOD_ROWS = 8
MOD_TN = 512


def _mod_row(i, tm):
    start = i * tm
    return jnp.where(start < N_CTX, 0, 1 + (start - N_CTX) // DEC_SEQ)


def _mod_kernel(c_ref, w_ref, b_ref, o_ref):
    cv = c_ref[...]
    s = (cv * jax.nn.sigmoid(cv)).astype(_BF)
    o_ref[...] = _dot(s, w_ref[...].astype(_BF)) + b_ref[...]


def _modulation(cvec, w_mod, b_mod):
    n = 6 * D_MODEL
    out = pl.pallas_call(
        _mod_kernel,
        grid=(DEPTH, n // MOD_TN),
        in_specs=[pl.BlockSpec((MOD_ROWS, D_MODEL), lambda l, j: (0, 0)),
                  pl.BlockSpec((None, D_MODEL, MOD_TN), lambda l, j: (l, 0, j)),
                  pl.BlockSpec((None, 1, MOD_TN), lambda l, j: (l, 0, j))],
        out_specs=pl.BlockSpec((None, MOD_ROWS, MOD_TN), lambda l, j: (l, 0, j)),
        out_shape=jax.ShapeDtypeStruct((DEPTH, MOD_ROWS, n), _F32),
        compiler_params=_cparams(("parallel", "parallel")),
        name="modulation",
    )(cvec, w_mod, b_mod.reshape(DEPTH, 1, n))
    return out.reshape(DEPTH, MOD_ROWS, 6, 1, D_MODEL).transpose(0, 2, 1, 3, 4)


NORM_TM = 512


def _norm_kernel(x_ref, g_ref, *rest, modulate):
    x = x_ref[...]
    y = x * lax.rsqrt(jnp.mean(x * x, axis=-1, keepdims=True) + EPS) * g_ref[...]
    if modulate:
        sc_ref, sh_ref, o_ref = rest
        y = y * (1.0 + sc_ref[...]) + sh_ref[...]
    else:
        (o_ref,) = rest
    o_ref[...] = y.astype(o_ref.dtype)


def _norm(x, g, mods=None, sc_idx=0, sh_idx=0):
    modulate = mods is not None
    tm = NORM_TM
    in_specs = [pl.BlockSpec((tm, D_MODEL), lambda i: (i, 0)),
                pl.BlockSpec((1, D_MODEL), lambda i: (0, 0))]
    args = [x, g.reshape(1, D_MODEL)]
    if modulate:
        in_specs += [pl.BlockSpec((None, None, 1, D_MODEL), lambda i: (sc_idx, _mod_row(i, tm), 0, 0)),
                     pl.BlockSpec((None, None, 1, D_MODEL), lambda i: (sh_idx, _mod_row(i, tm), 0, 0))]
        args += [mods, mods]
    return pl.pallas_call(
        functools.partial(_norm_kernel, modulate=modulate),
        grid=(N_TOK // tm,),
        in_specs=in_specs,
        out_specs=pl.BlockSpec((tm, D_MODEL), lambda i: (i, 0)),
        out_shape=jax.ShapeDtypeStruct((N_TOK, D_MODEL), _BF if modulate else _F32),
        compiler_params=_cparams(("parallel",)),
        name="norm_mod" if modulate else "norm_final",
    )(*args)


def _mm_kernel(a_ref, b_ref, *rest, epilogue, nk):
    if epilogue == "resid":
        x_ref, gt_ref, o_ref = rest[:3]
        rest = rest[3:]
    else:
        o_ref = rest[0]
        rest = rest[1:]

    def finish(acc):
        if epilogue == "resid":
            o_ref[...] = x_ref[...] + gt_ref[...] * acc
        elif epilogue == "relu2":
            o_ref[...] = jnp.square(jnp.maximum(acc, 0.0)).astype(o_ref.dtype)
        else:
            o_ref[...] = acc.astype(o_ref.dtype)

    if nk == 1:
        finish(_dot(a_ref[...], b_ref[...]))
        return
    (acc_ref,) = rest
    k = pl.program_id(2)

    @pl.when(k == 0)
    def _():
        acc_ref[...] = jnp.zeros_like(acc_ref)

    acc_ref[...] += _dot(a_ref[...], b_ref[...])

    @pl.when(k == nk - 1)
    def _():
        finish(acc_ref[...])


def _matmul(a, b, *, tm, tn, tk, epilogue="plain", out_dtype=_F32, x=None, mods=None, gt_idx=0, name="matmul"):
    M, K = a.shape
    N = b.shape[1]
    nk = K // tk
    in_specs = [pl.BlockSpec((tm, tk), lambda i, j, k: (i, k)),
                pl.BlockSpec((tk, tn), lambda i, j, k: (k, j))]
    args = [a, b]
    if epilogue == "resid":
        in_specs += [pl.BlockSpec((tm, tn), lambda i, j, k: (i, j)),
                     pl.BlockSpec((None, None, 1, tn), lambda i, j, k: (gt_idx, _mod_row(i, tm), 0, j))]
        args += [x, mods]
    return pl.pallas_call(
        functools.partial(_mm_kernel, epilogue=epilogue, nk=nk),
        grid=(M // tm, N // tn, nk),
        in_specs=in_specs,
        out_specs=pl.BlockSpec((tm, tn), lambda i, j, k: (i, j)),
        out_shape=jax.ShapeDtypeStruct((M, N), out_dtype),
        scratch_shapes=[pltpu.VMEM((tm, tn), _F32)] if nk > 1 else [],
        compiler_params=_cparams(("parallel", "parallel", "arbitrary")),
        name=name,
    )(*args)


ATT_TQ = 256
ATT_SCALE = HEAD_DIM ** -0.5


def _rope(x, cos, sin):
    lane = lax.broadcasted_iota(jnp.int32, x.shape, 1)
    first = (lane % (2 * ROPE_FREQS)) < ROPE_FREQS
    partner = jnp.where(first, pltpu.roll(x, HEAD_DIM - ROPE_FREQS, 1), pltpu.roll(x, ROPE_FREQS, 1))
    return x * cos + partner * sin


def _rope_tables(L):
    pos = jnp.arange(L)
    rows = (pos // GRID_W).astype(_F32)
    cols = (pos % GRID_W).astype(_F32)
    inv = ROPE_BASE ** (-jnp.arange(ROPE_FREQS, dtype=_F32) / ROPE_FREQS)
    ang_r = rows[:, None] * inv
    ang_c = cols[:, None] * inv
    cos = jnp.concatenate([jnp.cos(ang_r)] * 2 + [jnp.cos(ang_c)] * 2, axis=-1)
    sin = jnp.concatenate([-jnp.sin(ang_r), jnp.sin(ang_r), -jnp.sin(ang_c), jnp.sin(ang_c)], axis=-1)
    return cos, sin


def _softmax(s):
    m = jnp.max(s, axis=-1, keepdims=True)
    e = jnp.exp(s - m)
    return e * (1.0 / jnp.sum(e, axis=-1, keepdims=True))


def _rms_lanes(x, g):
    return x * lax.rsqrt(jnp.mean(x * x, axis=-1, keepdims=True) + EPS) * g


def _diff_kernel(*refs, L, latent, lam_init):
    it = iter(refs)
    q_ref, k_ref, v_ref = next(it), next(it), next(it)
    if latent:
        ck_ref, cv_ref, cos_ref, sin_ref = next(it), next(it), next(it), next(it)
    lam_ref, sub_ref, o_ref, kbuf, vbuf = next(it), next(it), next(it), next(it), next(it)

    lp = lam_ref[...]
    lam = (jnp.exp(jnp.sum(lp[0:1] * lp[1:2], axis=-1, keepdims=True))
           - jnp.exp(jnp.sum(lp[2:3] * lp[3:4], axis=-1, keepdims=True)) + lam_init)
    for cpt in range(2):
        cs = slice(cpt * HEAD_DIM, (cpt + 1) * HEAD_DIM)
        kc = k_ref[:, cs]
        if latent:
            kc = _rope(kc, cos_ref[...], sin_ref[...])
            kbuf[cpt, L:, :] = ck_ref[:, cs].astype(_BF)
        kbuf[cpt, 0:L, :] = kc.astype(_BF)
    vbuf[0:L, :] = v_ref[...].astype(_BF)
    if latent:
        vbuf[L:, :] = cv_ref[...].astype(_BF)

    for qi in range(L // ATT_TQ):
        rows = slice(qi * ATT_TQ, (qi + 1) * ATT_TQ)
        ps = []
        for cpt in range(2):
            qc = q_ref[rows, cpt * HEAD_DIM:(cpt + 1) * HEAD_DIM]
            if latent:
                qc = _rope(qc, cos_ref[rows, :], sin_ref[rows, :])
            ps.append(_softmax(_dot_nt(qc.astype(_BF), kbuf[cpt]) * ATT_SCALE))
        a = (ps[0] - lam * ps[1]).astype(_BF)
        o = _dot(a, vbuf[...])
        o_ref[rows, :] = (_rms_lanes(o, sub_ref[...]) * (1.0 - lam_init)).astype(o_ref.dtype)


def _diff_attn(proj, lam_p, subln, lam_init, *, row0, nb, L, layer, cache_k=None, cache_v=None, rope=None):
    latent = cache_k is not None
    rb0 = row0 // L
    w = 2 * HEAD_DIM
    nkeys = L + (PAST_LEN if latent else 0)

    def col(off):
        return pl.BlockSpec((L, w), lambda b, h, off=off: (rb0 + b, off // w + h))

    in_specs = [col(OFF_QA), col(OFF_KA), col(OFF_VA)]
    args = [proj, proj, proj]
    if latent:
        cspec = pl.BlockSpec((None, None, PAST_LEN, w), lambda b, h: (b, layer, 0, h))
        tspec = pl.BlockSpec((L, HEAD_DIM), lambda b, h: (0, 0))
        in_specs += [cspec, cspec, tspec, tspec]
        args += [cache_k.reshape(nb, DEPTH, PAST_LEN, GROUP_W), cache_v.reshape(nb, DEPTH, PAST_LEN, GROUP_W), rope[0], rope[1]]
    in_specs += [pl.BlockSpec((4, HEAD_DIM), lambda b, h: (0, 0)), pl.BlockSpec((1, w), lambda b, h: (0, 0))]
    args += [lam_p, subln.reshape(1, w)]
    return pl.pallas_call(
        functools.partial(_diff_kernel, L=L, latent=latent, lam_init=lam_init),
        grid=(nb, A_HEADS),
        in_specs=in_specs,
        out_specs=pl.BlockSpec((L, w), lambda b, h: (b, h)),
        out_shape=jax.ShapeDtypeStruct((nb * L, GROUP_W), _BF),
        scratch_shapes=[pltpu.VMEM((2, nkeys, HEAD_DIM), _BF), pltpu.VMEM((nkeys, w), _BF)],
        compiler_params=_cparams(("parallel", "parallel")),
        name="diff_attn_lat" if latent else "diff_attn_ctx",
    )(*args)


def _gqa_kernel(*refs, L, latent):
    it = iter(refs)
    q_ref, k_ref, v_ref = next(it), next(it), next(it)
    if latent:
        ck_ref, cv_ref, cos_ref, sin_ref = next(it), next(it), next(it), next(it)
    qn_ref, kn_ref, o_ref = next(it), next(it), next(it)
    kout_ref = None if latent else next(it)
    kbuf, vbuf = next(it), next(it)

    kn = _rms_lanes(k_ref[...], kn_ref[...])
    if latent:
        kn = _rope(kn, cos_ref[...], sin_ref[...])
        kbuf[L:, :] = ck_ref[...].astype(_BF)
        vbuf[L:, :] = cv_ref[...].astype(_BF)
    else:
        kout_ref[...] = kn
    kbuf[0:L, :] = kn.astype(_BF)
    vbuf[0:L, :] = v_ref[...].astype(_BF)

    for g in range(D_GROUPS):
        cs = slice(g * HEAD_DIM, (g + 1) * HEAD_DIM)
        for qi in range(L // ATT_TQ):
            rows = slice(qi * ATT_TQ, (qi + 1) * ATT_TQ)
            qg = _rms_lanes(q_ref[rows, cs], qn_ref[...])
            if latent:
                qg = _rope(qg, cos_ref[rows, :], sin_ref[rows, :])
            p = _softmax(_dot_nt(qg.astype(_BF), kbuf[...]) * ATT_SCALE)
            o_ref[rows, cs] = _dot(p.astype(_BF), vbuf[...]).astype(o_ref.dtype)


def _gqa(proj, qnorm, knorm, *, row0, nb, L, layer, cache_k=None, cache_v=None, rope=None):
    latent = cache_k is not None
    rb0 = row0 // L
    wq = D_GROUPS * HEAD_DIM
    nkeys = L + (PAST_LEN if latent else 0)
    in_specs = [pl.BlockSpec((L, wq), lambda b, n: (rb0 + b, OFF_QD // wq + n)),
                pl.BlockSpec((L, HEAD_DIM), lambda b, n: (rb0 + b, OFF_KD // HEAD_DIM + n)),
                pl.BlockSpec((L, HEAD_DIM), lambda b, n: (rb0 + b, OFF_VD // HEAD_DIM + n))]
    args = [proj, proj, proj]
    if latent:
        cspec = pl.BlockSpec((None, None, PAST_LEN, HEAD_DIM), lambda b, n: (b, layer, 0, n))
        tspec = pl.BlockSpec((L, HEAD_DIM), lambda b, n: (0, 0))
        in_specs += [cspec, cspec, tspec, tspec]
        kvw = D_KV_HEADS * HEAD_DIM
        args += [cache_k.reshape(nb, DEPTH, PAST_LEN, kvw), cache_v.reshape(nb, DEPTH, PAST_LEN, kvw), rope[0], rope[1]]
    gspec = pl.BlockSpec((1, HEAD_DIM), lambda b, n: (0, 0))
    in_specs += [gspec, gspec]
    args += [qnorm.reshape(1, HEAD_DIM), knorm.reshape(1, HEAD_DIM)]
    out_shape = [jax.ShapeDtypeStruct((nb * L, GROUP_W), _BF)]
    out_specs = [pl.BlockSpec((L, wq), lambda b, n: (b, n))]
    if not latent:
        out_shape.append(jax.ShapeDtypeStruct((nb * L, D_KV_HEADS * HEAD_DIM), _F32))
        out_specs.append(pl.BlockSpec((L, HEAD_DIM), lambda b, n: (b, n)))
    res = pl.pallas_call(
        functools.partial(_gqa_kernel, L=L, latent=latent),
        grid=(nb, D_KV_HEADS),
        in_specs=in_specs,
        out_specs=out_specs,
        out_shape=out_shape,
        scratch_shapes=[pltpu.VMEM((nkeys, HEAD_DIM), _BF), pltpu.VMEM((nkeys, HEAD_DIM), _BF)],
        compiler_params=_cparams(("parallel", "parallel")),
        name="gqa_lat" if latent else "gqa_ctx",
    )(*args)
    return (res[0], None) if latent else (res[0], res[1])


HY_CB = 256


def _dft_mats(L):
    f = np.arange(L, dtype=np.float64)[:, None] + 0.5
    t = np.arange(L, dtype=np.float64)[None, :]
    ang = 2.0 * np.pi * f * t / (2 * L)
    out = []
    for m in (np.cos(ang), np.sin(ang), np.cos(ang).T, np.sin(ang).T):
        m32 = jnp.asarray(m.astype(np.float32))
        hi = m32.astype(_BF)
        out += [hi, (m32 - hi.astype(_F32)).astype(_BF)]
    return tuple(out)


def _split_bf16(x):
    hi = x.astype(_BF)
    return hi, (x - hi.astype(_F32)).astype(_BF)


def _dot3(m_hi, m_lo, x):
    xh, xl = _split_bf16(x)
    return _dot(m_hi, xh) + (_dot(m_hi, xl) + _dot(m_lo, xh))


def _filter_features(L):
    t = jnp.arange(L, dtype=_F32) / L
    bands = jnp.arange(1, HY_NFREQ + 1, dtype=_F32)
    ang = 2.0 * math.pi * t[:, None] * bands
    z = jnp.concatenate([t[:, None], jnp.cos(ang), jnp.sin(ang)], axis=-1)
    return z, t[:, None]


def _filt_kernel(z_ref, t_ref, w1_ref, b1_ref, w2_ref, b2_ref, w3f_ref, w3b_ref, fr_ref, df_ref, db_ref, sk_ref,
                 cmh_ref, cml_ref, smh_ref, sml_ref, hr_ref, hi_ref, *, L):
    fr = fr_ref[...]
    hid = jnp.sin(fr * (_dot(z_ref[...], w1_ref[...], _HI) + b1_ref[...]))
    hid = jnp.sin(fr * (_dot(hid, w2_ref[...], _HI) + b2_ref[...]))
    t = t_ref[...]

    def filt(w3_ref, d_ref):
        h = _dot(hid, w3_ref[...], _HI) * (jnp.exp(-t * jnp.abs(d_ref[...])) + HY_SHIFT)
        return h / (jnp.sum(jnp.abs(h), axis=0, keepdims=True) + EPS)

    hf = filt(w3f_ref, df_ref)
    hb = filt(w3b_ref, db_ref)
    hr_ref[...] = (_dot3(cmh_ref[...], cml_ref[...], hf + hb) + sk_ref[...]) * (1.0 / L)
    hi_ref[...] = _dot3(smh_ref[...], sml_ref[...], hb - hf) * (1.0 / L)


def _const_spec(shape):
    zeros = (0,) * len(shape)
    return pl.BlockSpec(shape, lambda *_: zeros, pipeline_mode=pl.Buffered(1))


def _hyena_filters(L, mats, w1, b1, w2, b2, w3, freq, delta, skip):
    z, t = _filter_features(L)
    cb = HY_CB
    ncb = B_W // cb
    full = lambda shape: pl.BlockSpec(shape, lambda l, o, j: (0,) * len(shape))
    per_layer = lambda shape: pl.BlockSpec((None,) + shape, lambda l, o, j: (l,) + (0,) * len(shape))
    mspec = _const_spec((L, L))
    in_specs = [full((L, HY_EMB)), full((L, 1)),
                per_layer((HY_EMB, HY_FFN)), per_layer((1, HY_FFN)),
                per_layer((HY_FFN, HY_FFN)), per_layer((1, HY_FFN)),
                pl.BlockSpec((None, HY_FFN, cb), lambda l, o, j: (l, 0, o * ncb + j)),
                pl.BlockSpec((None, HY_FFN, cb), lambda l, o, j: (l, 0, (2 + o) * ncb + j)),
                per_layer((1, HY_FFN)),
                pl.BlockSpec((None, None, 1, cb), lambda l, o, j: (l, o, 0, j)),
                pl.BlockSpec((None, None, 1, cb), lambda l, o, j: (l, 2 + o, 0, j)),
                pl.BlockSpec((None, None, 1, cb), lambda l, o, j: (l, o, 0, j)),
                mspec, mspec, mspec, mspec]
    out_spec = pl.BlockSpec((None, None, L, cb), lambda l, o, j: (l, o, 0, j))
    out_sds = jax.ShapeDtypeStruct((DEPTH, 2, L, B_W), _F32)
    return pl.pallas_call(
        functools.partial(_filt_kernel, L=L),
        grid=(DEPTH, 2, ncb),
        in_specs=in_specs,
        out_specs=[out_spec, out_spec],
        out_shape=[out_sds, out_sds],
        compiler_params=_cparams(("parallel", "parallel", "parallel")),
        name=f"hyena_filters_{L}",
    )(z, t, w1, b1.reshape(DEPTH, 1, HY_FFN), w2, b2.reshape(DEPTH, 1, HY_FFN), w3, w3,
      freq.reshape(DEPTH, 1, HY_FFN), delta.reshape(DEPTH, 4, 1, B_W), delta.reshape(DEPTH, 4, 1, B_W),
      skip.reshape(DEPTH, 2, 1, B_W), *mats[:4])


def _short_conv(u, w, b):
    L = u.shape[0]
    row = lax.broadcasted_iota(jnp.int32, u.shape, 0)
    prev = jnp.where(row == 0, 0.0, pltpu.roll(u, 1, 0))
    nxt = jnp.where(row == L - 1, 0.0, pltpu.roll(u, L - 1, 0))
    return prev * w[0:1, :] + u * w[1:2, :] + nxt * w[2:3, :] + b


def _hyena_kernel(v_ref, x1_ref, x2_ref, wv_ref, w1_ref, w2_ref, bv_ref, b1_ref, b2_ref, hr_ref, hi_ref,
                  cmh_ref, cml_ref, smh_ref, sml_ref, cmth_ref, cmtl_ref, smth_ref, smtl_ref, o_ref):
    def conv(u, o):
        a = _dot3(cmh_ref[...], cml_ref[...], u)
        b = _dot3(smh_ref[...], sml_ref[...], u)
        hr = hr_ref[o]
        hi = hi_ref[o]
        return (_dot3(cmth_ref[...], cmtl_ref[...], a * hr + b * hi)
                - _dot3(smth_ref[...], smtl_ref[...], a * hi - b * hr))

    v = _short_conv(v_ref[...], wv_ref[...], bv_ref[...])
    z = _short_conv(x1_ref[...], w1_ref[...], b1_ref[...]) * conv(v, 0)
    y = _short_conv(x2_ref[...], w2_ref[...], b2_ref[...]) * conv(z, 1)
    o_ref[...] = y.astype(o_ref.dtype)


def _hyena(proj, conv_w, conv_b, hr, hi, mats, *, row0, nb, L, layer):
    rb0 = row0 // L
    cb = HY_CB
    ncb = B_W // cb

    def ucol(part):
        return pl.BlockSpec((L, cb), lambda b, j, part=part: (rb0 + b, OFF_UB // cb + part * ncb + j))

    def wcol(rows, part):
        return pl.BlockSpec((rows, cb), lambda b, j, part=part: (0, part * ncb + j))

    hspec = pl.BlockSpec((None, 2, L, cb), lambda b, j: (layer, 0, 0, j))
    mspec = _const_spec((L, L))
    return pl.pallas_call(
        _hyena_kernel,
        grid=(nb, ncb),
        in_specs=[ucol(0), ucol(1), ucol(2), wcol(3, 0), wcol(3, 1), wcol(3, 2), wcol(1, 0), wcol(1, 1), wcol(1, 2),
                  hspec, hspec] + [mspec] * 8,
        out_specs=pl.BlockSpec((L, cb), lambda b, j: (b, j)),
        out_shape=jax.ShapeDtypeStruct((nb * L, B_W), _BF),
        compiler_params=_cparams(("parallel", "parallel")),
        name=f"hyena_{L}",
    )(proj, proj, proj, conv_w, conv_w, conv_w, conv_b, conv_b, conv_b, hr, hi, *mats)


def _mixers(proj, l, lam_init, lp, *, row0, nb, L, caches, rope, filt, mats):
    kw = dict(row0=row0, nb=nb, L=L, layer=l)
    latent = caches is not None
    if latent:
        oa = _diff_attn(proj, lp["a_lam"], lp["a_subln"], lam_init, cache_k=caches[0], cache_v=caches[1], rope=rope, **kw)
        od, kd = _gqa(proj, lp["d_qnorm"], lp["d_knorm"], cache_k=caches[2], cache_v=caches[3], rope=rope, **kw)
        oc, st = _hgrn2(proj, lp["c_lb"], lp["c_gnorm"], caches[4], **kw)
    else:
        oa = _diff_attn(proj, lp["a_lam"], lp["a_subln"], lam_init, **kw)
        od, kd = _gqa(proj, lp["d_qnorm"], lp["d_knorm"], **kw)
        oc, st = _hgrn2(proj, lp["c_lb"], lp["c_gnorm"], None, **kw)
    ob = _hyena(proj, lp["b_conv_w"], lp["b_conv_b"], filt[0], filt[1], mats, **kw)
    return jnp.concatenate([oa, ob, oc, od], axis=-1), kd, st


def kernel(x_prompt, x_sample, c, cache_a_k, cache_a_v, cache_d_k, cache_d_v, state_c, c_ctx, w_mod, b_mod, g_norm1, g_norm2, w_in, w_out, a_lam, a_subln, b_conv_w, b_conv_b, b_ffn_w1, b_ffn_b1, b_ffn_w2, b_ffn_b2, b_ffn_w3, b_freq, b_delta, b_skip, c_lb_raw, c_gnorm, d_qnorm, d_knorm, w_mlp1, w_mlp2, g_final):
    p = jax.nn.softmax(c_lb_raw.astype(_F32), axis=0)
    lb_all = jnp.cumsum(p, axis=0) - p[:1]

    cvec = jnp.concatenate([c_ctx[None, :], c, jnp.zeros((MOD_ROWS - 1 - DEC_BATCH, D_MODEL), _F32)], axis=0)
    mods = _modulation(cvec, w_mod, b_mod)

    w_in_b, w_out_b = w_in.astype(_BF), w_out.astype(_BF)
    w_mlp1_b, w_mlp2_b = w_mlp1.astype(_BF), w_mlp2.astype(_BF)

    mats_c, mats_l = _dft_mats(SEQ), _dft_mats(DEC_SEQ)
    fargs = (b_ffn_w1, b_ffn_b1, b_ffn_w2, b_ffn_b2, b_ffn_w3, b_freq, b_delta, b_skip)
    filt_c = _hyena_filters(SEQ, mats_c, *fargs)
    filt_l = _hyena_filters(DEC_SEQ, mats_l, *fargs)
    rope = _rope_tables(DEC_SEQ)

    x = jnp.concatenate([x_prompt.reshape(N_CTX, D_MODEL), x_sample.reshape(N_LAT, D_MODEL)], axis=0)
    list_ak, list_av, list_dk, list_dv, list_s = [], [], [], [], []
    for l in range(DEPTH):
        lam_init = 0.8 - 0.6 * math.exp(-0.3 * l)
        lp = dict(a_lam=a_lam[l], a_subln=a_subln[l], b_conv_w=b_conv_w[l], b_conv_b=b_conv_b[l].reshape(1, 3 * B_W),
                  c_lb=lb_all[l], c_gnorm=c_gnorm[l].reshape(1, C_DV), d_qnorm=d_qnorm[l], d_knorm=d_knorm[l])
        h = _norm(x, g_norm1[l], mods[l], sc_idx=1, sh_idx=0)
        proj = _matmul(h, w_in_b[l], tm=1024, tn=512, tk=D_MODEL, name="in_proj")
        mix_c, kd, st = _mixers(proj, l, lam_init, lp, row0=0, nb=BATCH, L=SEQ, caches=None, rope=None,
                                filt=filt_c, mats=mats_c)
        mix_l, _, _ = _mixers(proj, l, lam_init, lp, row0=N_CTX, nb=DEC_BATCH, L=DEC_SEQ,
                              caches=(cache_a_k, cache_a_v, cache_d_k, cache_d_v, state_c), rope=rope,
                              filt=filt_l, mats=mats_l)
        mix = jnp.concatenate([mix_c, mix_l], axis=0)
        x = _matmul(mix, w_out_b[l], tm=1024, tn=512, tk=D_MODEL, epilogue="resid", x=x, mods=mods[l], gt_idx=2,
                    name="out_proj")
        h2 = _norm(x, g_norm2[l], mods[l], sc_idx=4, sh_idx=3)
        hid = _matmul(h2, w_mlp1_b[l], tm=1024, tn=512, tk=D_MODEL, epilogue="relu2", out_dtype=_BF, name="mlp1")
        x = _matmul(hid, w_mlp2_b[l], tm=1024, tn=1024, tk=2048, epilogue="resid", x=x, mods=mods[l], gt_idx=5,
                    name="mlp2")

        pc = proj[:N_CTX]
        list_ak.append(pc[:, OFF_KA:OFF_KA + GROUP_W].reshape(BATCH, SEQ, A_HEADS, 2 * HEAD_DIM))
        list_av.append(pc[:, OFF_VA:OFF_VA + GROUP_W].reshape(BATCH, SEQ, A_HEADS, 2 * HEAD_DIM))
        list_dk.append(kd.reshape(BATCH, SEQ, D_KV_HEADS, HEAD_DIM))
        list_dv.append(pc[:, OFF_VD:OFF_VD + D_KV_HEADS * HEAD_DIM].reshape(BATCH, SEQ, D_KV_HEADS, HEAD_DIM))
        list_s.append(st)

    y = _norm(x, g_final)
    y_prompt = y[:N_CTX].reshape(BATCH, SEQ, D_MODEL)
    y_sample = y[N_CTX:].reshape(DEC_BATCH, DEC_SEQ, D_MODEL)
    return (y_prompt, y_sample, jnp.stack(list_ak, axis=1), jnp.stack(list_av, axis=1),
            jnp.stack(list_dk, axis=1), jnp.stack(list_dv, axis=1), jnp.stack(list_s, axis=1))
```

```python
import functools
import math

import jax
import jax.numpy as jnp
import numpy as np
from jax import lax
from jax.experimental import pallas as pl
from jax.experimental.pallas import tpu as pltpu

D_MODEL = 4096
BATCH = 32
SEQ = 256
DEPTH = 4
DEC_BATCH = 4
DEC_SEQ = 1024
PAST_LEN = 512
GRID_W = 64
HEAD_DIM = 128
GROUP_W = D_MODEL // 4
A_HEADS = GROUP_W // (2 * HEAD_DIM)
B_W = GROUP_W
C_HEADS = 8
C_DK = 128
C_DV = GROUP_W // C_HEADS
C_W = C_HEADS * C_DK
D_HEADS = GROUP_W // HEAD_DIM
D_KV_HEADS = 2
D_GROUPS = D_HEADS // D_KV_HEADS
D_FF = 4 * D_MODEL
ROPE_BASE = 10000.0
ROPE_FREQS = HEAD_DIM // 4
HY_NFREQ = 16
HY_EMB = 1 + 2 * HY_NFREQ
HY_FFN = 64
HY_SHIFT = 0.05
EPS = 1e-6
GATE_FLOOR = 1e-20

N_CTX = BATCH * SEQ
N_LAT = DEC_BATCH * DEC_SEQ
N_TOK = N_CTX + N_LAT

OFF_QA = 0
OFF_KA = OFF_QA + GROUP_W
OFF_VA = OFF_KA + GROUP_W
OFF_UB = OFF_VA + GROUP_W
OFF_QC = OFF_UB + 3 * B_W
OFF_IC = OFF_QC + C_W
OFF_FF = OFF_IC + C_HEADS * C_DV
OFF_FB = OFF_FF + C_W
OFF_GZ = OFF_FB + C_W
OFF_QD = OFF_GZ + C_HEADS * C_DV
OFF_KD = OFF_QD + D_HEADS * HEAD_DIM
OFF_VD = OFF_KD + D_KV_HEADS * HEAD_DIM
IN_W = OFF_VD + D_KV_HEADS * HEAD_DIM

LANES = 128
SUBLANES = 8
VMEM_LIMIT = 56 * 1024 * 1024

CHUNK = 128

_HI = lax.Precision.HIGHEST
_BF = jnp.bfloat16
_F32 = jnp.float32


def _cparams(sem):
    return pltpu.CompilerParams(dimension_semantics=sem, vmem_limit_bytes=VMEM_LIMIT)


def _call_into(kernel_fn, bufs, *, grid, in_specs, args, out_specs, out_shape, scratch_shapes, sem, name):
    held = [b for b in bufs if b is not None]
    n = len(held)
    aliases = {}
    for oi, b in enumerate(bufs):
        if b is not None:
            aliases[len(aliases)] = oi

    def body(*refs):
        kernel_fn(*refs[n:])

    return pl.pallas_call(
        body,
        grid=grid,
        in_specs=[pl.BlockSpec(memory_space=pl.ANY)] * n + list(in_specs),
        out_specs=list(out_specs),
        out_shape=list(out_shape),
        scratch_shapes=list(scratch_shapes),
        input_output_aliases=aliases,
        compiler_params=_cparams(sem),
        name=name,
    )(*held, *args)


def _dot_nt(a, b):
    return lax.dot_general(a, b, (((1,), (1,)), ((), ())), preferred_element_type=_F32)


def _dot_tn(a, b):
    return lax.dot_general(a, b, (((0,), (0,)), ((), ())), preferred_element_type=_F32)


def _dot(a, b, precision=None):
    return jnp.dot(a, b, preferred_element_type=_F32, precision=precision)


HGRN_LEVELS = tuple(2 ** i for i in range(1, CHUNK.bit_length()))
LOG2E = 1.4426950408889634


def _hgrn_consts():
    C = CHUNK
    t = np.arange(C)[:, None]
    s = np.arange(C)[None, :]
    sgn = np.zeros((2, len(HGRN_LEVELS), C, LANES), np.float32)
    mask = np.zeros((2, len(HGRN_LEVELS) + 1, C, C), np.float32)
    tri = np.zeros((2, C, C), np.float32)
    for d in range(2):
        for li, blk in enumerate(HGRN_LEVELS):
            half = blk // 2
            q_t = ((t % blk) >= half) if d == 0 else ((t % blk) < half)
            q_s = ((s % blk) >= half) if d == 0 else ((s % blk) < half)
            sgn[d, li] = np.where(q_t, 1.0, -1.0)
            mask[d, li] = ((t // blk) == (s // blk)) & q_t & ~q_s
        mask[d, -1] = (t == s)
        tri[d] = (s <= t) if d == 0 else (s >= t)
    return jnp.asarray(tri, _BF), jnp.asarray(sgn), jnp.asarray(mask)


def _ref_rows(G, blk, reverse):
    C = G.shape[0]
    half = blk // 2
    rr = half if reverse else half - 1
    if blk >= SUBLANES:
        GB = G.reshape(C // blk, blk, LANES)
        return jnp.broadcast_to(GB[:, rr:rr + 1, :], GB.shape).reshape(C, LANES)
    if blk == 4:
        G3 = G.reshape(C // SUBLANES, SUBLANES, LANES)
        sub = lax.broadcasted_iota(jnp.int32, G3.shape, 1)
        lo = jnp.broadcast_to(G3[:, rr:rr + 1, :], G3.shape)
        hi = jnp.broadcast_to(G3[:, rr + 4:rr + 5, :], G3.shape)
        return jnp.where(sub < 4, lo, hi).reshape(C, LANES)
    odd = (lax.broadcasted_iota(jnp.int32, G.shape, 0) & 1) == 1
    if reverse:
        return jnp.where(odd, G, pltpu.roll(G, C - 1, 0))
    return jnp.where(odd, pltpu.roll(G, 1, 0), G)


def _hgrn_intra(qq, kk, vb, lf, d, tri_ref, sgn_ref, mask_ref):
    C = CHUNK
    l1 = lf.astype(_BF)
    r1 = lf - l1.astype(_F32)
    l2 = r1.astype(_BF)
    l3 = (r1 - l2.astype(_F32)).astype(_BF)
    g3 = _dot(tri_ref[d], jnp.concatenate([l1, l2, l3], axis=1))
    G2 = (g3[:, 0:LANES] + (g3[:, LANES:2 * LANES] + g3[:, 2 * LANES:3 * LANES])) * LOG2E
    scores = mask_ref[d, len(HGRN_LEVELS)] * _dot_nt(qq.astype(_BF), kk.astype(_BF))
    for li, blk in enumerate(HGRN_LEVELS):
        sg = sgn_ref[d, li]
        e = jnp.exp2((G2 - _ref_rows(G2, blk, d == 1)) * sg)
        x = (jnp.where(sg > 0.0, qq, kk) * e).astype(_BF)
        scores = scores + mask_ref[d, li] * _dot_nt(x, x)
    o = _dot(scores.astype(_BF), vb)
    qdec = (qq * jnp.exp2(G2)).astype(_BF)
    gend = G2[0:1, :] if d == 1 else G2[C - 1:C, :]
    kdec = (kk * jnp.exp2(gend - G2)).astype(_BF)
    return o, qdec, _dot_tn(vb, kdec), jnp.exp2(gend)


def _chunk_loop(n, body):
    if n <= 2:
        for c in range(n):
            body(c, 0)
    else:
        lax.fori_loop(0, n, body, 0, unroll=2)


def _hgrn2_kernel(*refs, L, has_state, emit_state):
    it = iter(refs)
    q_ref, v_ref, zf_ref, zb_ref, gz_ref, lb_ref, gn_ref, tri_ref, sgn_ref, mask_ref = (next(it) for _ in range(10))
    s0_ref = next(it) if has_state else None
    o_ref = next(it)
    sout_ref = next(it) if emit_state else None
    oacc, qdec_buf, ubuf, dbuf, st_ref = (next(it) for _ in range(5))
    nchunk = L // CHUNK

    def rows_of(c):
        start = c * CHUNK
        return pl.ds(start if isinstance(c, int) else pl.multiple_of(start, CHUNK), CHUNK)

    def local(c, carry):
        rows = rows_of(c)
        qq = q_ref[rows, :]
        qq = qq * jax.nn.sigmoid(qq)
        vb = v_ref[rows, :].astype(_BF)
        for d, z_ref in enumerate((zf_ref, zb_ref)):
            z = z_ref[rows, :]
            lb = lb_ref[d:d + 1, :]
            lf = jnp.log(jnp.maximum(lb + (1.0 - lb) * jax.nn.sigmoid(z), GATE_FLOOR))
            kk = (1.0 - lb) * jax.nn.sigmoid(-z)
            o, qd, u, dec = _hgrn_intra(qq, kk, vb, lf, d, tri_ref, sgn_ref, mask_ref)
            oacc[d, rows, :] = o
            qdec_buf[d, rows, :] = qd
            ubuf[d, c] = u
            dbuf[d, c] = dec
        return carry

    _chunk_loop(nchunk, local)

    for d in range(2):
        st_ref[d] = s0_ref[d].T if has_state else jnp.zeros((C_DV, C_DK), _F32)

    def scan(i, carry):
        for d in range(2):
            c = (nchunk - 1 - i) if d == 1 else i
            rows = rows_of(c)
            st = st_ref[d]
            oacc[d, rows, :] = oacc[d, rows, :] + _dot_nt(qdec_buf[d, rows, :], st.astype(_BF))
            st_ref[d] = st * dbuf[d, c] + ubuf[d, c]
        return carry

    _chunk_loop(nchunk, scan)
    if emit_state:
        for d in range(2):
            sout_ref[d] = st_ref[d].T

    def finish(c, carry):
        rows = rows_of(c)
        tot = oacc[0, rows, :] + oacc[1, rows, :]
        g = gz_ref[rows, :]
        o_ref[rows, :] = (_rms_lanes(tot, gn_ref[...]) * (g * jax.nn.sigmoid(g))).astype(o_ref.dtype)
        return carry

    _chunk_loop(nchunk, finish)


MIX_COL = dict(a=0, b=GROUP_W, c=2 * GROUP_W, d=3 * GROUP_W)
MIX_SDS = jax.ShapeDtypeStruct((N_TOK, D_MODEL), _BF)


def _hgrn2(proj, lb, gnorm, s0, mix, states, *, row0, nb, L, layer):
    has_state = s0 is not None
    emit_state = not has_state
    rb0 = row0 // L
    nchunk = L // CHUNK
    tri, sgn, mask = _hgrn_consts()

    def col(off):
        return pl.BlockSpec((L, LANES), lambda b, h, off=off: (rb0 + b, off // LANES + h))

    in_specs = [col(OFF_QC), col(OFF_IC), col(OFF_FF), col(OFF_FB), col(OFF_GZ),
                pl.BlockSpec((2, LANES), lambda b, h: (0, h)),
                pl.BlockSpec((1, C_DV), lambda b, h: (0, 0)),
                _const_spec(tri.shape), _const_spec(sgn.shape), _const_spec(mask.shape)]
    args = [proj, proj, proj, proj, proj, lb, gnorm, tri, sgn, mask]
    if has_state:
        in_specs.append(pl.BlockSpec((None, None, 2, None, C_DK, C_DV), lambda b, h: (b, layer, 0, h, 0, 0)))
        args.append(s0)
    out_shape = [MIX_SDS]
    out_specs = [pl.BlockSpec((L, LANES), lambda b, h: (rb0 + b, MIX_COL["c"] // LANES + h))]
    bufs = [mix]
    if emit_state:
        out_shape.append(jax.ShapeDtypeStruct((nb, DEPTH, 2, C_HEADS, C_DK, C_DV), _F32))
        out_specs.append(pl.BlockSpec((None, None, 2, None, C_DK, C_DV), lambda b, h: (b, layer, 0, h, 0, 0)))
        bufs.append(states)
    res = _call_into(
        functools.partial(_hgrn2_kernel, L=L, has_state=has_state, emit_state=emit_state), bufs,
        grid=(nb, C_HEADS),
        in_specs=in_specs,
        args=args,
        out_specs=out_specs,
        out_shape=out_shape,
        scratch_shapes=[pltpu.VMEM((2, L, C_DV), _F32), pltpu.VMEM((2, L, C_DK), _BF),
                        pltpu.VMEM((2, nchunk, C_DV, C_DK), _F32), pltpu.VMEM((2, nchunk, 1, C_DK), _F32),
                        pltpu.VMEM((2, C_DV, C_DK), _F32)],
        sem=("parallel", "parallel"),
        name="hgrn2_lat" if has_state else "hgrn2_ctx",
    )
    return (res[0], res[1]) if emit_state else (res[0], states)


MOD_ROWS = 8
MOD_TN = 512


def _mod_row(i, tm):
    start = i * tm
    return jnp.where(start < N_CTX, 0, 1 + (start - N_CTX) // DEC_SEQ)


def _mod_kernel(c_ref, w_ref, b_ref, o_ref):
    cv = c_ref[...]
    s = (cv * jax.nn.sigmoid(cv)).astype(_BF)
    o_ref[...] = _dot(s, w_ref[...].astype(_BF)) + b_ref[...]


def _modulation(cvec, w_mod, b_mod):
    n = 6 * D_MODEL
    out = pl.pallas_call(
        _mod_kernel,
        grid=(DEPTH, n // MOD_TN),
        in_specs=[pl.BlockSpec((MOD_ROWS, D_MODEL), lambda l, j: (0, 0)),
                  pl.BlockSpec((None, D_MODEL, MOD_TN), lambda l, j: (l, 0, j)),
                  pl.BlockSpec((None, 1, MOD_TN), lambda l, j: (l, 0, j))],
        out_specs=pl.BlockSpec((None, MOD_ROWS, MOD_TN), lambda l, j: (l, 0, j)),
        out_shape=jax.ShapeDtypeStruct((DEPTH, MOD_ROWS, n), _F32),
        compiler_params=_cparams(("parallel", "parallel")),
        name="modulation",
    )(cvec, w_mod, b_mod.reshape(DEPTH, 1, n))
    return out.reshape(DEPTH, MOD_ROWS, 6, 1, D_MODEL).transpose(0, 2, 1, 3, 4)


NORM_TM = 512


def _norm_kernel(x_ref, g_ref, *rest, modulate):
    x = x_ref[...]
    y = x * lax.rsqrt(jnp.mean(x * x, axis=-1, keepdims=True) + EPS) * g_ref[...]
    if modulate:
        sc_ref, sh_ref, o_ref = rest
        y = y * (1.0 + sc_ref[...]) + sh_ref[...]
    else:
        (o_ref,) = rest
    o_ref[...] = y.astype(o_ref.dtype)


def _norm(x, g, mods=None, sc_idx=0, sh_idx=0, row0=0, nrows=N_TOK):
    modulate = mods is not None
    tm = NORM_TM
    t0 = row0 // tm
    in_specs = [pl.BlockSpec((tm, D_MODEL), lambda i: (t0 + i, 0)),
                pl.BlockSpec((1, D_MODEL), lambda i: (0, 0))]
    args = [x, g.reshape(1, D_MODEL)]
    if modulate:
        in_specs += [pl.BlockSpec((None, None, 1, D_MODEL), lambda i: (sc_idx, _mod_row(t0 + i, tm), 0, 0)),
                     pl.BlockSpec((None, None, 1, D_MODEL), lambda i: (sh_idx, _mod_row(t0 + i, tm), 0, 0))]
        args += [mods, mods]
    return pl.pallas_call(
        functools.partial(_norm_kernel, modulate=modulate),
        grid=(nrows // tm,),
        in_specs=in_specs,
        out_specs=pl.BlockSpec((tm, D_MODEL), lambda i: (i, 0)),
        out_shape=jax.ShapeDtypeStruct((nrows, D_MODEL), _BF if modulate else _F32),
        compiler_params=_cparams(("parallel",)),
        name="norm_mod" if modulate else "norm_final",
    )(*args)


def _mm_kernel(a_ref, b_ref, *rest, epilogue, nk):
    if epilogue == "resid":
        x_ref, gt_ref, o_ref = rest[:3]
        rest = rest[3:]
    else:
        o_ref = rest[0]
        rest = rest[1:]

    def finish(acc):
        if epilogue == "resid":
            o_ref[...] = x_ref[...] + gt_ref[...] * acc
        elif epilogue == "relu2":
            o_ref[...] = jnp.square(jnp.maximum(acc, 0.0)).astype(o_ref.dtype)
        else:
            o_ref[...] = acc.astype(o_ref.dtype)

    if nk == 1:
        finish(_dot(a_ref[...], b_ref[...]))
        return
    (acc_ref,) = rest
    k = pl.program_id(2)

    @pl.when(k == 0)
    def _():
        acc_ref[...] = jnp.zeros_like(acc_ref)

    acc_ref[...] += _dot(a_ref[...], b_ref[...])

    @pl.when(k == nk - 1)
    def _():
        finish(acc_ref[...])


MM_TILES = dict(in_proj=(1024, 1280, D_MODEL), out_proj=(1024, 512, D_MODEL),
                mlp1=(1024, 1024, D_MODEL), mlp2=(1024, 1024, 2048))


def _matmul(a, b, *, name, epilogue="plain", out_dtype=_F32, x=None, mods=None, gt_idx=0):
    M, K = a.shape
    N = b.shape[1]
    tm, tn, tk = MM_TILES[name]
    nk = K // tk
    in_specs = [pl.BlockSpec((tm, tk), lambda i, j, k: (i, k)),
                pl.BlockSpec((tk, tn), lambda i, j, k: (k, j))]
    args = [a, b]
    if epilogue == "resid":
        in_specs += [pl.BlockSpec((tm, tn), lambda i, j, k: (i, j)),
                     pl.BlockSpec((None, None, 1, tn), lambda i, j, k: (gt_idx, _mod_row(i, tm), 0, j))]
        args += [x, mods]
    return pl.pallas_call(
        functools.partial(_mm_kernel, epilogue=epilogue, nk=nk),
        grid=(M // tm, N // tn, nk),
        in_specs=in_specs,
        out_specs=pl.BlockSpec((tm, tn), lambda i, j, k: (i, j)),
        out_shape=jax.ShapeDtypeStruct((M, N), out_dtype),
        scratch_shapes=[pltpu.VMEM((tm, tn), _F32)] if nk > 1 else [],
        compiler_params=_cparams(("parallel", "parallel", "arbitrary")),
        name=name,
    )(*args)


ATT_TQ = 256
ATT_SCALE = HEAD_DIM ** -0.5


def _rope(x, cos, sin):
    lane = lax.broadcasted_iota(jnp.int32, x.shape, 1)
    first = (lane % (2 * ROPE_FREQS)) < ROPE_FREQS
    partner = jnp.where(first, pltpu.roll(x, HEAD_DIM - ROPE_FREQS, 1), pltpu.roll(x, ROPE_FREQS, 1))
    return x * cos + partner * sin


def _rope_tables(L):
    pos = jnp.arange(L)
    rows = (pos // GRID_W).astype(_F32)
    cols = (pos % GRID_W).astype(_F32)
    inv = ROPE_BASE ** (-jnp.arange(ROPE_FREQS, dtype=_F32) / ROPE_FREQS)
    ang_r = rows[:, None] * inv
    ang_c = cols[:, None] * inv
    cos = jnp.concatenate([jnp.cos(ang_r)] * 2 + [jnp.cos(ang_c)] * 2, axis=-1)
    sin = jnp.concatenate([-jnp.sin(ang_r), jnp.sin(ang_r), -jnp.sin(ang_c), jnp.sin(ang_c)], axis=-1)
    return cos, sin


def _softmax(s):
    m = jnp.max(s, axis=-1, keepdims=True)
    e = jnp.exp(s - m)
    return e * (1.0 / jnp.sum(e, axis=-1, keepdims=True))


def _rms_lanes(x, g):
    return x * lax.rsqrt(jnp.mean(x * x, axis=-1, keepdims=True) + EPS) * g


def _diff_kernel(*refs, L, latent, lam_init):
    it = iter(refs)
    q_ref, k_ref, v_ref = next(it), next(it), next(it)
    if latent:
        ck_ref, cv_ref, cos_ref, sin_ref = next(it), next(it), next(it), next(it)
    lam_ref, sub_ref, o_ref = next(it), next(it), next(it)
    if not latent:
        ak_ref, av_ref = next(it), next(it)
    kbuf, vbuf = next(it), next(it)
    w = 2 * HEAD_DIM

    lp = lam_ref[...]
    lam = (jnp.exp(jnp.sum(lp[0:1] * lp[1:2], axis=-1, keepdims=True))
           - jnp.exp(jnp.sum(lp[2:3] * lp[3:4], axis=-1, keepdims=True)) + lam_init)
    for h in range(1 if latent else A_HEADS):
        base = h * w
        for cpt in range(2):
            cs = slice(base + cpt * HEAD_DIM, base + (cpt + 1) * HEAD_DIM)
            kc = k_ref[:, cs]
            if latent:
                kc = _rope(kc, cos_ref[...], sin_ref[...])
                kbuf[cpt, L:, :] = ck_ref[:, cs].astype(_BF)
            kbuf[cpt, 0:L, :] = kc.astype(_BF)
        vv = v_ref[:, base:base + w]
        vbuf[0:L, :] = vv.astype(_BF)
        if latent:
            vbuf[L:, :] = cv_ref[...].astype(_BF)
        else:
            ak_ref[:, h, :] = k_ref[:, base:base + w]
            av_ref[:, h, :] = vv

        for qi in range(L // ATT_TQ):
            rows = slice(qi * ATT_TQ, (qi + 1) * ATT_TQ)
            ps = []
            for cpt in range(2):
                qc = q_ref[rows, base + cpt * HEAD_DIM:base + (cpt + 1) * HEAD_DIM]
                if latent:
                    qc = _rope(qc, cos_ref[rows, :], sin_ref[rows, :])
                ps.append(_softmax(_dot_nt(qc.astype(_BF), kbuf[cpt]) * ATT_SCALE))
            a = (ps[0] - lam * ps[1]).astype(_BF)
            o = _dot(a, vbuf[...])
            o_ref[rows, base:base + w] = (_rms_lanes(o, sub_ref[...]) * (1.0 - lam_init)).astype(o_ref.dtype)


def _diff_attn(proj, lam_p, subln, lam_init, mix, new_k, new_v, *, row0, nb, L, layer,
               cache_k=None, cache_v=None, rope=None):
    latent = cache_k is not None
    rb0 = row0 // L
    w = 2 * HEAD_DIM
    nkeys = L + (PAST_LEN if latent else 0)
    cw = w if latent else GROUP_W

    def col(off):
        return pl.BlockSpec((L, cw), lambda b, h, off=off: (rb0 + b, off // cw + h))

    in_specs = [col(OFF_QA), col(OFF_KA), col(OFF_VA)]
    args = [proj, proj, proj]
    if latent:
        cspec = pl.BlockSpec((None, None, PAST_LEN, w), lambda b, h: (b, layer, 0, h))
        tspec = pl.BlockSpec((L, HEAD_DIM), lambda b, h: (0, 0))
        in_specs += [cspec, cspec, tspec, tspec]
        args += [cache_k.reshape(nb, DEPTH, PAST_LEN, GROUP_W), cache_v.reshape(nb, DEPTH, PAST_LEN, GROUP_W), rope[0], rope[1]]
    in_specs += [pl.BlockSpec((4, HEAD_DIM), lambda b, h: (0, 0)), pl.BlockSpec((1, w), lambda b, h: (0, 0))]
    args += [lam_p, subln.reshape(1, w)]
    out_shape = [MIX_SDS]
    out_specs = [pl.BlockSpec((L, cw), lambda b, h: (rb0 + b, MIX_COL["a"] // cw + h))]
    bufs = [mix]
    if not latent:
        kv_sds = jax.ShapeDtypeStruct((nb, DEPTH, L, A_HEADS, w), _F32)
        kv_spec = pl.BlockSpec((None, None, L, A_HEADS, w), lambda b, h: (b, layer, 0, 0, 0))
        out_shape += [kv_sds, kv_sds]
        out_specs += [kv_spec, kv_spec]
        bufs += [new_k, new_v]
    res = _call_into(
        functools.partial(_diff_kernel, L=L, latent=latent, lam_init=lam_init), bufs,
        grid=(nb, A_HEADS if latent else 1),
        in_specs=in_specs,
        args=args,
        out_specs=out_specs,
        out_shape=out_shape,
        scratch_shapes=[pltpu.VMEM((2, nkeys, HEAD_DIM), _BF), pltpu.VMEM((nkeys, w), _BF)],
        sem=("parallel", "parallel"),
        name="diff_attn_lat" if latent else "diff_attn_ctx",
    )
    return (res[0], new_k, new_v) if latent else tuple(res)


def _gqa_kernel(*refs, L, latent):
    it = iter(refs)
    q_ref, k_ref, v_ref = next(it), next(it), next(it)
    if latent:
        ck_ref, cv_ref, cos_ref, sin_ref = next(it), next(it), next(it), next(it)
    qn_ref, kn_ref, o_ref = next(it), next(it), next(it)
    if not latent:
        dk_ref, dv_ref = next(it), next(it)
    kbuf, vbuf = next(it), next(it)
    wq = D_GROUPS * HEAD_DIM

    for n in range(1 if latent else D_KV_HEADS):
        ks = slice(n * HEAD_DIM, (n + 1) * HEAD_DIM)
        kn = _rms_lanes(k_ref[:, ks], kn_ref[...])
        vv = v_ref[:, ks]
        if latent:
            kn = _rope(kn, cos_ref[...], sin_ref[...])
            kbuf[L:, :] = ck_ref[...].astype(_BF)
            vbuf[L:, :] = cv_ref[...].astype(_BF)
        else:
            dk_ref[:, n, :] = kn
            dv_ref[:, n, :] = vv
        kbuf[0:L, :] = kn.astype(_BF)
        vbuf[0:L, :] = vv.astype(_BF)

        for g in range(D_GROUPS):
            cs = slice(n * wq + g * HEAD_DIM, n * wq + (g + 1) * HEAD_DIM)
            for qi in range(L // ATT_TQ):
                rows = slice(qi * ATT_TQ, (qi + 1) * ATT_TQ)
                qg = _rms_lanes(q_ref[rows, cs], qn_ref[...])
                if latent:
                    qg = _rope(qg, cos_ref[rows, :], sin_ref[rows, :])
                p = _softmax(_dot_nt(qg.astype(_BF), kbuf[...]) * ATT_SCALE)
                o_ref[rows, cs] = _dot(p.astype(_BF), vbuf[...]).astype(o_ref.dtype)


def _gqa(proj, qnorm, knorm, mix, new_k, new_v, *, row0, nb, L, layer, cache_k=None, cache_v=None, rope=None):
    latent = cache_k is not None
    rb0 = row0 // L
    nh = 1 if latent else D_KV_HEADS
    wq = nh * D_GROUPS * HEAD_DIM
    wk = nh * HEAD_DIM
    nkeys = L + (PAST_LEN if latent else 0)
    in_specs = [pl.BlockSpec((L, wq), lambda b, n: (rb0 + b, OFF_QD // wq + n)),
                pl.BlockSpec((L, wk), lambda b, n: (rb0 + b, OFF_KD // wk + n)),
                pl.BlockSpec((L, wk), lambda b, n: (rb0 + b, OFF_VD // wk + n))]
    args = [proj, proj, proj]
    if latent:
        cspec = pl.BlockSpec((None, None, PAST_LEN, HEAD_DIM), lambda b, n: (b, layer, 0, n))
        tspec = pl.BlockSpec((L, HEAD_DIM), lambda b, n: (0, 0))
        in_specs += [cspec, cspec, tspec, tspec]
        kvw = D_KV_HEADS * HEAD_DIM
        args += [cache_k.reshape(nb, DEPTH, PAST_LEN, kvw), cache_v.reshape(nb, DEPTH, PAST_LEN, kvw), rope[0], rope[1]]
    gspec = pl.BlockSpec((1, HEAD_DIM), lambda b, n: (0, 0))
    in_specs += [gspec, gspec]
    args += [qnorm.reshape(1, HEAD_DIM), knorm.reshape(1, HEAD_DIM)]
    out_shape = [MIX_SDS]
    out_specs = [pl.BlockSpec((L, wq), lambda b, n: (rb0 + b, MIX_COL["d"] // wq + n))]
    bufs = [mix]
    if not latent:
        kv_sds = jax.ShapeDtypeStruct((nb, DEPTH, L, D_KV_HEADS, HEAD_DIM), _F32)
        kv_spec = pl.BlockSpec((None, None, L, D_KV_HEADS, HEAD_DIM), lambda b, n: (b, layer, 0, 0, 0))
        out_shape += [kv_sds, kv_sds]
        out_specs += [kv_spec, kv_spec]
        bufs += [new_k, new_v]
    res = _call_into(
        functools.partial(_gqa_kernel, L=L, latent=latent), bufs,
        grid=(nb, D_KV_HEADS // nh),
        in_specs=in_specs,
        args=args,
        out_specs=out_specs,
        out_shape=out_shape,
        scratch_shapes=[pltpu.VMEM((nkeys, HEAD_DIM), _BF), pltpu.VMEM((nkeys, HEAD_DIM), _BF)],
        sem=("parallel", "parallel"),
        name="gqa_lat" if latent else "gqa_ctx",
    )
    return (res[0], new_k, new_v) if latent else tuple(res)


HY_CB = 256


def _dft_mats(L):
    f = np.arange(L, dtype=np.float64)[:, None] + 0.5
    t = np.arange(L, dtype=np.float64)[None, :]
    ang = 2.0 * np.pi * f * t / (2 * L)
    out = []
    for m in (np.cos(ang), np.sin(ang), np.cos(ang).T, np.sin(ang).T):
        m32 = jnp.asarray(m.astype(np.float32))
        hi = m32.astype(_BF)
        out += [hi, (m32 - hi.astype(_F32)).astype(_BF)]
    return tuple(out)


def _split_bf16(x):
    hi = x.astype(_BF)
    return hi, (x - hi.astype(_F32)).astype(_BF)


def _dot3(m_hi, m_lo, x):
    xh, xl = _split_bf16(x)
    return _dot(m_hi, xh) + (_dot(m_hi, xl) + _dot(m_lo, xh))


def _filter_features(L):
    t = jnp.arange(L, dtype=_F32) / L
    bands = jnp.arange(1, HY_NFREQ + 1, dtype=_F32)
    ang = 2.0 * math.pi * t[:, None] * bands
    z = jnp.concatenate([t[:, None], jnp.cos(ang), jnp.sin(ang)], axis=-1)
    return z, t[:, None]


def _filt_kernel(z_ref, t_ref, w1_ref, b1_ref, w2_ref, b2_ref, w3f_ref, w3b_ref, fr_ref, df_ref, db_ref, sk_ref,
                 cmh_ref, cml_ref, smh_ref, sml_ref, hr_ref, hi_ref, *, L):
    fr = fr_ref[...]
    hid = jnp.sin(fr * (_dot(z_ref[...], w1_ref[...], _HI) + b1_ref[...]))
    hid = jnp.sin(fr * (_dot(hid, w2_ref[...], _HI) + b2_ref[...]))
    t = t_ref[...]

    def filt(w3_ref, d_ref):
        h = _dot(hid, w3_ref[...], _HI) * (jnp.exp(-t * jnp.abs(d_ref[...])) + HY_SHIFT)
        return h / (jnp.sum(jnp.abs(h), axis=0, keepdims=True) + EPS)

    hf = filt(w3f_ref, df_ref)
    hb = filt(w3b_ref, db_ref)
    hr_ref[...] = (_dot3(cmh_ref[...], cml_ref[...], hf + hb) + sk_ref[...]) * (1.0 / L)
    hi_ref[...] = _dot3(smh_ref[...], sml_ref[...], hb - hf) * (1.0 / L)


def _const_spec(shape):
    zeros = (0,) * len(shape)
    return pl.BlockSpec(shape, lambda *_: zeros, pipeline_mode=pl.Buffered(1))


def _hyena_filters(L, mats, w1, b1, w2, b2, w3, freq, delta, skip):
    z, t = _filter_features(L)
    cb = HY_CB
    ncb = B_W // cb
    full = lambda shape: pl.BlockSpec(shape, lambda l, o, j: (0,) * len(shape))
    per_layer = lambda shape: pl.BlockSpec((None,) + shape, lambda l, o, j: (l,) + (0,) * len(shape))
    mspec = _const_spec((L, L))
    in_specs = [full((L, HY_EMB)), full((L, 1)),
                per_layer((HY_EMB, HY_FFN)), per_layer((1, HY_FFN)),
                per_layer((HY_FFN, HY_FFN)), per_layer((1, HY_FFN)),
                pl.BlockSpec((None, HY_FFN, cb), lambda l, o, j: (l, 0, o * ncb + j)),
                pl.BlockSpec((None, HY_FFN, cb), lambda l, o, j: (l, 0, (2 + o) * ncb + j)),
                per_layer((1, HY_FFN)),
                pl.BlockSpec((None, None, 1, cb), lambda l, o, j: (l, o, 0, j)),
                pl.BlockSpec((None, None, 1, cb), lambda l, o, j: (l, 2 + o, 0, j)),
                pl.BlockSpec((None, None, 1, cb), lambda l, o, j: (l, o, 0, j)),
                mspec, mspec, mspec, mspec]
    out_spec = pl.BlockSpec((None, None, L, cb), lambda l, o, j: (l, o, 0, j))
    out_sds = jax.ShapeDtypeStruct((DEPTH, 2, L, B_W), _F32)
    return pl.pallas_call(
        functools.partial(_filt_kernel, L=L),
        grid=(DEPTH, 2, ncb),
        in_specs=in_specs,
        out_specs=[out_spec, out_spec],
        out_shape=[out_sds, out_sds],
        compiler_params=_cparams(("parallel", "parallel", "parallel")),
        name=f"hyena_filters_{L}",
    )(z, t, w1, b1.reshape(DEPTH, 1, HY_FFN), w2, b2.reshape(DEPTH, 1, HY_FFN), w3, w3,
      freq.reshape(DEPTH, 1, HY_FFN), delta.reshape(DEPTH, 4, 1, B_W), delta.reshape(DEPTH, 4, 1, B_W),
      skip.reshape(DEPTH, 2, 1, B_W), *mats[:4])


def _short_conv(u, w, b):
    L = u.shape[0]
    row = lax.broadcasted_iota(jnp.int32, u.shape, 0)
    prev = jnp.where(row == 0, 0.0, pltpu.roll(u, 1, 0))
    nxt = jnp.where(row == L - 1, 0.0, pltpu.roll(u, L - 1, 0))
    return prev * w[0:1, :] + u * w[1:2, :] + nxt * w[2:3, :] + b


def _hyena_kernel(v_ref, x1_ref, x2_ref, wv_ref, w1_ref, w2_ref, bv_ref, b1_ref, b2_ref, hr_ref, hi_ref,
                  cmh_ref, cml_ref, smh_ref, sml_ref, cmth_ref, cmtl_ref, smth_ref, smtl_ref, o_ref):
    def conv(u, o):
        a = _dot3(cmh_ref[...], cml_ref[...], u)
        b = _dot3(smh_ref[...], sml_ref[...], u)
        hr = hr_ref[o]
        hi = hi_ref[o]
        return (_dot3(cmth_ref[...], cmtl_ref[...], a * hr + b * hi)
                - _dot3(smth_ref[...], smtl_ref[...], a * hi - b * hr))

    v = _short_conv(v_ref[...], wv_ref[...], bv_ref[...])
    z = _short_conv(x1_ref[...], w1_ref[...], b1_ref[...]) * conv(v, 0)
    y = _short_conv(x2_ref[...], w2_ref[...], b2_ref[...]) * conv(z, 1)
    o_ref[...] = y.astype(o_ref.dtype)


def _hyena(proj, conv_w, conv_b, hr, hi, mats, mix, *, row0, nb, L, layer):
    rb0 = row0 // L
    cb = HY_CB
    ncb = B_W // cb

    def ucol(part):
        return pl.BlockSpec((L, cb), lambda b, j, part=part: (rb0 + b, OFF_UB // cb + part * ncb + j))

    def wcol(rows, part):
        return pl.BlockSpec((rows, cb), lambda b, j, part=part: (0, part * ncb + j))

    hspec = pl.BlockSpec((None, 2, L, cb), lambda b, j: (layer, 0, 0, j))
    mspec = _const_spec((L, L))
    return _call_into(
        _hyena_kernel, [mix],
        grid=(nb, ncb),
        in_specs=[ucol(0), ucol(1), ucol(2), wcol(3, 0), wcol(3, 1), wcol(3, 2), wcol(1, 0), wcol(1, 1), wcol(1, 2),
                  hspec, hspec] + [mspec] * 8,
        args=[proj, proj, proj, conv_w, conv_w, conv_w, conv_b, conv_b, conv_b, hr, hi, *mats],
        out_specs=[pl.BlockSpec((L, cb), lambda b, j: (rb0 + b, MIX_COL["b"] // cb + j))],
        out_shape=[MIX_SDS],
        scratch_shapes=[],
        sem=("parallel", "parallel"),
        name=f"hyena_{L}",
    )[0]


def _mixers(proj, l, lam_init, lp, mix, outs, *, row0, nb, L, caches, rope, filt, mats):
    kw = dict(row0=row0, nb=nb, L=L, layer=l)
    ak, av, dk, dv, st = outs
    if caches is not None:
        att = dict(rope=rope, **kw)
        mix, _, _ = _diff_attn(proj, lp["a_lam"], lp["a_subln"], lam_init, mix, ak, av,
                               cache_k=caches[0], cache_v=caches[1], **att)
        mix, _, _ = _gqa(proj, lp["d_qnorm"], lp["d_knorm"], mix, dk, dv, cache_k=caches[2], cache_v=caches[3], **att)
        mix, _ = _hgrn2(proj, lp["c_lb"], lp["c_gnorm"], caches[4], mix, st, **kw)
    else:
        mix, ak, av = _diff_attn(proj, lp["a_lam"], lp["a_subln"], lam_init, mix, ak, av, **kw)
        mix, dk, dv = _gqa(proj, lp["d_qnorm"], lp["d_knorm"], mix, dk, dv, **kw)
        mix, st = _hgrn2(proj, lp["c_lb"], lp["c_gnorm"], None, mix, st, **kw)
    mix = _hyena(proj, lp["b_conv_w"], lp["b_conv_b"], filt[0], filt[1], mats, mix, **kw)
    return mix, (ak, av, dk, dv, st)


def kernel(x_prompt, x_sample, c, cache_a_k, cache_a_v, cache_d_k, cache_d_v, state_c, c_ctx, w_mod, b_mod, g_norm1, g_norm2, w_in, w_out, a_lam, a_subln, b_conv_w, b_conv_b, b_ffn_w1, b_ffn_b1, b_ffn_w2, b_ffn_b2, b_ffn_w3, b_freq, b_delta, b_skip, c_lb_raw, c_gnorm, d_qnorm, d_knorm, w_mlp1, w_mlp2, g_final):
    p = jax.nn.softmax(c_lb_raw.astype(_F32), axis=0)
    lb_all = jnp.cumsum(p, axis=0) - p[:1]

    cvec = jnp.concatenate([c_ctx[None, :], c, jnp.zeros((MOD_ROWS - 1 - DEC_BATCH, D_MODEL), _F32)], axis=0)
    mods = _modulation(cvec, w_mod, b_mod)

    w_in_b, w_out_b = w_in.astype(_BF), w_out.astype(_BF)
    w_mlp1_b, w_mlp2_b = w_mlp1.astype(_BF), w_mlp2.astype(_BF)

    mats_c, mats_l = _dft_mats(SEQ), _dft_mats(DEC_SEQ)
    fargs = (b_ffn_w1, b_ffn_b1, b_ffn_w2, b_ffn_b2, b_ffn_w3, b_freq, b_delta, b_skip)
    filt_c = _hyena_filters(SEQ, mats_c, *fargs)
    filt_l = _hyena_filters(DEC_SEQ, mats_l, *fargs)
    rope = _rope_tables(DEC_SEQ)

    x = jnp.concatenate([x_prompt.reshape(N_CTX, D_MODEL), x_sample.reshape(N_LAT, D_MODEL)], axis=0)
    outs = (None,) * 5
    for l in range(DEPTH):
        lam_init = 0.8 - 0.6 * math.exp(-0.3 * l)
        lp = dict(a_lam=a_lam[l], a_subln=a_subln[l], b_conv_w=b_conv_w[l], b_conv_b=b_conv_b[l].reshape(1, 3 * B_W),
                  c_lb=lb_all[l], c_gnorm=c_gnorm[l].reshape(1, C_DV), d_qnorm=d_qnorm[l], d_knorm=d_knorm[l])
        h = _norm(x, g_norm1[l], mods[l], sc_idx=1, sh_idx=0)
        proj = _matmul(h, w_in_b[l], name="in_proj")
        mix, outs = _mixers(proj, l, lam_init, lp, None, outs, row0=0, nb=BATCH, L=SEQ, caches=None, rope=None,
                            filt=filt_c, mats=mats_c)
        mix, outs = _mixers(proj, l, lam_init, lp, mix, outs, row0=N_CTX, nb=DEC_BATCH, L=DEC_SEQ,
                            caches=(cache_a_k, cache_a_v, cache_d_k, cache_d_v, state_c), rope=rope,
                            filt=filt_l, mats=mats_l)
        x = _matmul(mix, w_out_b[l], name="out_proj", epilogue="resid", x=x, mods=mods[l], gt_idx=2)
        h2 = _norm(x, g_norm2[l], mods[l], sc_idx=4, sh_idx=3)
        hid = _matmul(h2, w_mlp1_b[l], name="mlp1", epilogue="relu2", out_dtype=_BF)
        x = _matmul(hid, w_mlp2_b[l], name="mlp2", epilogue="resid", x=x, mods=mods[l], gt_idx=5)

    y_prompt = _norm(x, g_final, row0=0, nrows=N_CTX).reshape(BATCH, SEQ, D_MODEL)
    y_sample = _norm(x, g_final, row0=N_CTX, nrows=N_LAT).reshape(DEC_BATCH, DEC_SEQ, D_MODEL)
    return (y_prompt, y_sample) + outs
```

```python
import functools
import math

import jax
import jax.numpy as jnp
import numpy as np
from jax import lax
from jax.experimental import pallas as pl
from jax.experimental.pallas import tpu as pltpu

D_MODEL = 4096
BATCH = 32
SEQ = 256
DEPTH = 4
DEC_BATCH = 4
DEC_SEQ = 1024
PAST_LEN = 512
GRID_W = 64
HEAD_DIM = 128
GROUP_W = D_MODEL // 4
A_HEADS = GROUP_W // (2 * HEAD_DIM)
B_W = GROUP_W
C_HEADS = 8
C_DK = 128
C_DV = GROUP_W // C_HEADS
C_W = C_HEADS * C_DK
D_HEADS = GROUP_W // HEAD_DIM
D_KV_HEADS = 2
D_GROUPS = D_HEADS // D_KV_HEADS
D_FF = 4 * D_MODEL
ROPE_BASE = 10000.0
ROPE_FREQS = HEAD_DIM // 4
HY_NFREQ = 16
HY_EMB = 1 + 2 * HY_NFREQ
HY_FFN = 64
HY_SHIFT = 0.05
EPS = 1e-6
GATE_FLOOR = 1e-20

N_CTX = BATCH * SEQ
N_LAT = DEC_BATCH * DEC_SEQ
N_TOK = N_CTX + N_LAT

OFF_QA = 0
OFF_KA = OFF_QA + GROUP_W
OFF_VA = OFF_KA + GROUP_W
OFF_UB = OFF_VA + GROUP_W
OFF_QC = OFF_UB + 3 * B_W
OFF_IC = OFF_QC + C_W
OFF_FF = OFF_IC + C_HEADS * C_DV
OFF_FB = OFF_FF + C_W
OFF_GZ = OFF_FB + C_W
OFF_QD = OFF_GZ + C_HEADS * C_DV
OFF_KD = OFF_QD + D_HEADS * HEAD_DIM
OFF_VD = OFF_KD + D_KV_HEADS * HEAD_DIM
IN_W = OFF_VD + D_KV_HEADS * HEAD_DIM

LANES = 128
SUBLANES = 8
VMEM_LIMIT = 56 * 1024 * 1024

CHUNK = 128

_HI = lax.Precision.HIGHEST
_BF = jnp.bfloat16
_F32 = jnp.float32


def _cparams(sem):
    return pltpu.CompilerParams(dimension_semantics=sem, vmem_limit_bytes=VMEM_LIMIT)


def _call_into(kernel_fn, bufs, *, grid, in_specs, args, out_specs, out_shape, scratch_shapes, sem, name):
    held = [b for b in bufs if b is not None]
    n = len(held)
    aliases = {}
    for oi, b in enumerate(bufs):
        if b is not None:
            aliases[len(aliases)] = oi

    def body(*refs):
        kernel_fn(*refs[n:])

    return pl.pallas_call(
        body,
        grid=grid,
        in_specs=[pl.BlockSpec(memory_space=pl.ANY)] * n + list(in_specs),
        out_specs=list(out_specs),
        out_shape=list(out_shape),
        scratch_shapes=list(scratch_shapes),
        input_output_aliases=aliases,
        compiler_params=_cparams(sem),
        name=name,
    )(*held, *args)


def _dot_nt(a, b):
    return lax.dot_general(a, b, (((1,), (1,)), ((), ())), preferred_element_type=_F32)


def _dot_tn(a, b):
    return lax.dot_general(a, b, (((0,), (0,)), ((), ())), preferred_element_type=_F32)


def _dot(a, b, precision=None):
    return jnp.dot(a, b, preferred_element_type=_F32, precision=precision)


HGRN_LEVELS = tuple(2 ** i for i in range(1, CHUNK.bit_length()))
LOG2E = 1.4426950408889634


def _hgrn_consts():
    C = CHUNK
    t = np.arange(C)[:, None]
    s = np.arange(C)[None, :]
    sgn = np.zeros((2, len(HGRN_LEVELS), C, LANES), np.float32)
    mask = np.zeros((2, len(HGRN_LEVELS) + 1, C, C), np.float32)
    tri = np.zeros((2, C, C), np.float32)
    for d in range(2):
        for li, blk in enumerate(HGRN_LEVELS):
            half = blk // 2
            q_t = ((t % blk) >= half) if d == 0 else ((t % blk) < half)
            q_s = ((s % blk) >= half) if d == 0 else ((s % blk) < half)
            sgn[d, li] = np.where(q_t, 1.0, -1.0)
            mask[d, li] = ((t // blk) == (s // blk)) & q_t & ~q_s
        mask[d, -1] = (t == s)
        tri[d] = (s <= t) if d == 0 else (s >= t)
    return jnp.asarray(tri, _BF), jnp.asarray(sgn), jnp.asarray(mask)


def _ref_rows(G, blk, reverse):
    C = G.shape[0]
    half = blk // 2
    rr = half if reverse else half - 1
    if blk >= SUBLANES:
        GB = G.reshape(C // blk, blk, LANES)
        return jnp.broadcast_to(GB[:, rr:rr + 1, :], GB.shape).reshape(C, LANES)
    if blk == 4:
        G3 = G.reshape(C // SUBLANES, SUBLANES, LANES)
        sub = lax.broadcasted_iota(jnp.int32, G3.shape, 1)
        lo = jnp.broadcast_to(G3[:, rr:rr + 1, :], G3.shape)
        hi = jnp.broadcast_to(G3[:, rr + 4:rr + 5, :], G3.shape)
        return jnp.where(sub < 4, lo, hi).reshape(C, LANES)
    odd = (lax.broadcasted_iota(jnp.int32, G.shape, 0) & 1) == 1
    if reverse:
        return jnp.where(odd, G, pltpu.roll(G, C - 1, 0))
    return jnp.where(odd, pltpu.roll(G, 1, 0), G)


def _hgrn_intra(qq, kk, vb, lf, d, tri_ref, sgn_ref, mask_ref):
    C = CHUNK
    l1 = lf.astype(_BF)
    r1 = lf - l1.astype(_F32)
    l2 = r1.astype(_BF)
    l3 = (r1 - l2.astype(_F32)).astype(_BF)
    g3 = _dot(tri_ref[d], jnp.concatenate([l1, l2, l3], axis=1))
    G2 = (g3[:, 0:LANES] + (g3[:, LANES:2 * LANES] + g3[:, 2 * LANES:3 * LANES])) * LOG2E
    scores = mask_ref[d, len(HGRN_LEVELS)] * _dot_nt(qq.astype(_BF), kk.astype(_BF))
    for li, blk in enumerate(HGRN_LEVELS):
        sg = sgn_ref[d, li]
        e = jnp.exp2((G2 - _ref_rows(G2, blk, d == 1)) * sg)
        x = (jnp.where(sg > 0.0, qq, kk) * e).astype(_BF)
        scores = scores + mask_ref[d, li] * _dot_nt(x, x)
    o = _dot(scores.astype(_BF), vb)
    qdec = (qq * jnp.exp2(G2)).astype(_BF)
    gend = G2[0:1, :] if d == 1 else G2[C - 1:C, :]
    kdec = (kk * jnp.exp2(gend - G2)).astype(_BF)
    return o, qdec, _dot_tn(vb, kdec), jnp.exp2(gend)


def _chunk_loop(n, body):
    if n <= 2:
        for c in range(n):
            body(c, 0)
    else:
        lax.fori_loop(0, n, body, 0, unroll=2)


def _hgrn2_kernel(*refs, L, has_state, emit_state):
    it = iter(refs)
    q_ref, v_ref, zf_ref, zb_ref, gz_ref, lb_ref, gn_ref, tri_ref, sgn_ref, mask_ref = (next(it) for _ in range(10))
    s0_ref = next(it) if has_state else None
    o_ref = next(it)
    sout_ref = next(it) if emit_state else None
    oacc, qdec_buf, ubuf, dbuf, st_ref = (next(it) for _ in range(5))
    nchunk = L // CHUNK

    def rows_of(c):
        start = c * CHUNK
        return pl.ds(start if isinstance(c, int) else pl.multiple_of(start, CHUNK), CHUNK)

    def local(c, carry):
        rows = rows_of(c)
        qq = q_ref[rows, :]
        qq = qq * jax.nn.sigmoid(qq)
        vb = v_ref[rows, :].astype(_BF)
        for d, z_ref in enumerate((zf_ref, zb_ref)):
            z = z_ref[rows, :]
            lb = lb_ref[d:d + 1, :]
            lf = jnp.log(jnp.maximum(lb + (1.0 - lb) * jax.nn.sigmoid(z), GATE_FLOOR))
            kk = (1.0 - lb) * jax.nn.sigmoid(-z)
            o, qd, u, dec = _hgrn_intra(qq, kk, vb, lf, d, tri_ref, sgn_ref, mask_ref)
            oacc[d, rows, :] = o
            qdec_buf[d, rows, :] = qd
            ubuf[d, c] = u
            dbuf[d, c] = dec
        return carry

    _chunk_loop(nchunk, local)

    for d in range(2):
        st_ref[d] = s0_ref[d].T if has_state else jnp.zeros((C_DV, C_DK), _F32)

    def scan(i, carry):
        for d in range(2):
            c = (nchunk - 1 - i) if d == 1 else i
            rows = rows_of(c)
            st = st_ref[d]
            oacc[d, rows, :] = oacc[d, rows, :] + _dot_nt(qdec_buf[d, rows, :], st.astype(_BF))
            st_ref[d] = st * dbuf[d, c] + ubuf[d, c]
        return carry

    _chunk_loop(nchunk, scan)
    if emit_state:
        for d in range(2):
            sout_ref[d] = st_ref[d].T

    def finish(c, carry):
        rows = rows_of(c)
        tot = oacc[0, rows, :] + oacc[1, rows, :]
        g = gz_ref[rows, :]
        o_ref[rows, :] = (_rms_lanes(tot, gn_ref[...]) * (g * jax.nn.sigmoid(g))).astype(o_ref.dtype)
        return carry

    _chunk_loop(nchunk, finish)


MIX_COL = dict(a=0, b=GROUP_W, c=2 * GROUP_W, d=3 * GROUP_W)
MIX_SDS = jax.ShapeDtypeStruct((N_TOK, D_MODEL), _BF)


def _hgrn2(proj, lb, gnorm, s0, mix, states, *, row0, nb, L, layer):
    has_state = s0 is not None
    emit_state = not has_state
    rb0 = row0 // L
    nchunk = L // CHUNK
    tri, sgn, mask = _hgrn_consts()

    def col(off):
        return pl.BlockSpec((L, LANES), lambda b, h, off=off: (rb0 + b, off // LANES + h))

    in_specs = [col(OFF_QC), col(OFF_IC), col(OFF_FF), col(OFF_FB), col(OFF_GZ),
                pl.BlockSpec((2, LANES), lambda b, h: (0, h)),
                pl.BlockSpec((1, C_DV), lambda b, h: (0, 0)),
                _const_spec(tri.shape), _const_spec(sgn.shape), _const_spec(mask.shape)]
    args = [proj, proj, proj, proj, proj, lb, gnorm, tri, sgn, mask]
    if has_state:
        in_specs.append(pl.BlockSpec((None, None, 2, None, C_DK, C_DV), lambda b, h: (b, layer, 0, h, 0, 0)))
        args.append(s0)
    out_shape = [MIX_SDS]
    out_specs = [pl.BlockSpec((L, LANES), lambda b, h: (rb0 + b, MIX_COL["c"] // LANES + h))]
    bufs = [mix]
    if emit_state:
        out_shape.append(jax.ShapeDtypeStruct((nb, DEPTH, 2, C_HEADS, C_DK, C_DV), _F32))
        out_specs.append(pl.BlockSpec((None, None, 2, None, C_DK, C_DV), lambda b, h: (b, layer, 0, h, 0, 0)))
        bufs.append(states)
    res = _call_into(
        functools.partial(_hgrn2_kernel, L=L, has_state=has_state, emit_state=emit_state), bufs,
        grid=(nb, C_HEADS),
        in_specs=in_specs,
        args=args,
        out_specs=out_specs,
        out_shape=out_shape,
        scratch_shapes=[pltpu.VMEM((2, L, C_DV), _F32), pltpu.VMEM((2, L, C_DK), _BF),
                        pltpu.VMEM((2, nchunk, C_DV, C_DK), _F32), pltpu.VMEM((2, nchunk, 1, C_DK), _F32),
                        pltpu.VMEM((2, C_DV, C_DK), _F32)],
        sem=("parallel", "parallel"),
        name="hgrn2_lat" if has_state else "hgrn2_ctx",
    )
    return (res[0], res[1]) if emit_state else (res[0], states)


MOD_ROWS = 8
MOD_TN = 512


def _mod_row(i, tm):
    start = i * tm
    return jnp.where(start < N_CTX, 0, 1 + (start - N_CTX) // DEC_SEQ)


def _mod_kernel(c_ref, w_ref, b_ref, o_ref):
    cv = c_ref[...]
    s = (cv * jax.nn.sigmoid(cv)).astype(_BF)
    o_ref[...] = _dot(s, w_ref[...].astype(_BF)) + b_ref[...]


def _modulation(cvec, w_mod, b_mod):
    n = 6 * D_MODEL
    out = pl.pallas_call(
        _mod_kernel,
        grid=(DEPTH, n // MOD_TN),
        in_specs=[pl.BlockSpec((MOD_ROWS, D_MODEL), lambda l, j: (0, 0)),
                  pl.BlockSpec((None, D_MODEL, MOD_TN), lambda l, j: (l, 0, j)),
                  pl.BlockSpec((None, 1, MOD_TN), lambda l, j: (l, 0, j))],
        out_specs=pl.BlockSpec((None, MOD_ROWS, MOD_TN), lambda l, j: (l, 0, j)),
        out_shape=jax.ShapeDtypeStruct((DEPTH, MOD_ROWS, n), _F32),
        compiler_params=_cparams(("parallel", "parallel")),
        name="modulation",
    )(cvec, w_mod, b_mod.reshape(DEPTH, 1, n))
    return out.reshape(DEPTH, MOD_ROWS, 6, 1, D_MODEL).transpose(0, 2, 1, 3, 4)


NORM_TM = 512


def _norm_kernel(x_ref, g_ref, *rest, modulate):
    x = x_ref[...]
    y = x * lax.rsqrt(jnp.mean(x * x, axis=-1, keepdims=True) + EPS) * g_ref[...]
    if modulate:
        sc_ref, sh_ref, o_ref = rest
        y = y * (1.0 + sc_ref[...]) + sh_ref[...]
    else:
        (o_ref,) = rest
    o_ref[...] = y.astype(o_ref.dtype)


def _norm(x, g, mods=None, sc_idx=0, sh_idx=0, row0=0, nrows=N_TOK):
    modulate = mods is not None
    tm = NORM_TM
    t0 = row0 // tm
    in_specs = [pl.BlockSpec((tm, D_MODEL), lambda i: (t0 + i, 0)),
                pl.BlockSpec((1, D_MODEL), lambda i: (0, 0))]
    args = [x, g.reshape(1, D_MODEL)]
    if modulate:
        in_specs += [pl.BlockSpec((None, None, 1, D_MODEL), lambda i: (sc_idx, _mod_row(t0 + i, tm), 0, 0)),
                     pl.BlockSpec((None, None, 1, D_MODEL), lambda i: (sh_idx, _mod_row(t0 + i, tm), 0, 0))]
        args += [mods, mods]
    return pl.pallas_call(
        functools.partial(_norm_kernel, modulate=modulate),
        grid=(nrows // tm,),
        in_specs=in_specs,
        out_specs=pl.BlockSpec((tm, D_MODEL), lambda i: (i, 0)),
        out_shape=jax.ShapeDtypeStruct((nrows, D_MODEL), _BF if modulate else _F32),
        compiler_params=_cparams(("parallel",)),
        name="norm_mod" if modulate else "norm_final",
    )(*args)


def _mm_kernel(a_ref, b_ref, *rest, epilogue, nk):
    if epilogue == "resid":
        x_ref, gt_ref, o_ref = rest[:3]
        rest = rest[3:]
    else:
        o_ref = rest[0]
        rest = rest[1:]

    def finish(acc):
        if epilogue == "resid":
            o_ref[...] = x_ref[...] + gt_ref[...] * acc
        elif epilogue == "relu2":
            o_ref[...] = jnp.square(jnp.maximum(acc, 0.0)).astype(o_ref.dtype)
        else:
            o_ref[...] = acc.astype(o_ref.dtype)

    if nk == 1:
        finish(_dot(a_ref[...], b_ref[...]))
        return
    (acc_ref,) = rest
    k = pl.program_id(2)

    @pl.when(k == 0)
    def _():
        acc_ref[...] = jnp.zeros_like(acc_ref)

    acc_ref[...] += _dot(a_ref[...], b_ref[...])

    @pl.when(k == nk - 1)
    def _():
        finish(acc_ref[...])


MM_TILES = dict(in_proj=(1024, 1280, D_MODEL), out_proj=(1024, 1024, D_MODEL),
                mlp1=(1024, 1024, D_MODEL), mlp2=(1024, 1024, 2048))


def _matmul(a, b, layer, *, name, epilogue="plain", out_dtype=_F32, x=None, mods=None, gt_idx=0):
    M, K = a.shape
    N = b.shape[2]
    tm, tn, tk = MM_TILES[name]
    nk = K // tk
    in_specs = [pl.BlockSpec((tm, tk), lambda i, j, k: (i, k)),
                pl.BlockSpec((None, tk, tn), lambda i, j, k: (layer, k, j))]
    args = [a, b]
    if epilogue == "resid":
        in_specs += [pl.BlockSpec((tm, tn), lambda i, j, k: (i, j)),
                     pl.BlockSpec((None, None, 1, tn), lambda i, j, k: (gt_idx, _mod_row(i, tm), 0, j))]
        args += [x, mods]
    return pl.pallas_call(
        functools.partial(_mm_kernel, epilogue=epilogue, nk=nk),
        grid=(M // tm, N // tn, nk),
        in_specs=in_specs,
        out_specs=pl.BlockSpec((tm, tn), lambda i, j, k: (i, j)),
        out_shape=jax.ShapeDtypeStruct((M, N), out_dtype),
        scratch_shapes=[pltpu.VMEM((tm, tn), _F32)] if nk > 1 else [],
        compiler_params=_cparams(("parallel", "parallel", "arbitrary")),
        name=name,
    )(*args)


ATT_TQ = 256
ATT_SCALE = HEAD_DIM ** -0.5


def _rope(x, cos, sin):
    lane = lax.broadcasted_iota(jnp.int32, x.shape, 1)
    first = (lane % (2 * ROPE_FREQS)) < ROPE_FREQS
    partner = jnp.where(first, pltpu.roll(x, HEAD_DIM - ROPE_FREQS, 1), pltpu.roll(x, ROPE_FREQS, 1))
    return x * cos + partner * sin


def _rope_tables(L):
    pos = jnp.arange(L)
    rows = (pos // GRID_W).astype(_F32)
    cols = (pos % GRID_W).astype(_F32)
    inv = ROPE_BASE ** (-jnp.arange(ROPE_FREQS, dtype=_F32) / ROPE_FREQS)
    ang_r = rows[:, None] * inv
    ang_c = cols[:, None] * inv
    cos = jnp.concatenate([jnp.cos(ang_r)] * 2 + [jnp.cos(ang_c)] * 2, axis=-1)
    sin = jnp.concatenate([-jnp.sin(ang_r), jnp.sin(ang_r), -jnp.sin(ang_c), jnp.sin(ang_c)], axis=-1)
    return cos, sin


def _softmax(s):
    m = jnp.max(s, axis=-1, keepdims=True)
    e = jnp.exp(s - m)
    return e * (1.0 / jnp.sum(e, axis=-1, keepdims=True))


def _rms_lanes(x, g):
    return x * lax.rsqrt(jnp.mean(x * x, axis=-1, keepdims=True) + EPS) * g


def _diff_kernel(*refs, L, latent, lam_init):
    it = iter(refs)
    q_ref, k_ref, v_ref = next(it), next(it), next(it)
    if latent:
        ck_ref, cv_ref, cos_ref, sin_ref = next(it), next(it), next(it), next(it)
    lam_ref, sub_ref, o_ref = next(it), next(it), next(it)
    if not latent:
        ak_ref, av_ref = next(it), next(it)
    kbuf, vbuf = next(it), next(it)
    w = 2 * HEAD_DIM

    lp = lam_ref[...]
    lam = (jnp.exp(jnp.sum(lp[0:1] * lp[1:2], axis=-1, keepdims=True))
           - jnp.exp(jnp.sum(lp[2:3] * lp[3:4], axis=-1, keepdims=True)) + lam_init)
    for h in range(1 if latent else A_HEADS):
        base = h * w
        for cpt in range(2):
            cs = slice(base + cpt * HEAD_DIM, base + (cpt + 1) * HEAD_DIM)
            kc = k_ref[:, cs]
            if latent:
                kc = _rope(kc, cos_ref[...], sin_ref[...])
                kbuf[cpt, L:, :] = ck_ref[:, cs].astype(_BF)
            kbuf[cpt, 0:L, :] = kc.astype(_BF)
        vv = v_ref[:, base:base + w]
        vbuf[0:L, :] = vv.astype(_BF)
        if latent:
            vbuf[L:, :] = cv_ref[...].astype(_BF)
        else:
            ak_ref[:, h, :] = k_ref[:, base:base + w]
            av_ref[:, h, :] = vv

        for qi in range(L // ATT_TQ):
            rows = slice(qi * ATT_TQ, (qi + 1) * ATT_TQ)
            ps = []
            for cpt in range(2):
                qc = q_ref[rows, base + cpt * HEAD_DIM:base + (cpt + 1) * HEAD_DIM]
                if latent:
                    qc = _rope(qc, cos_ref[rows, :], sin_ref[rows, :])
                ps.append(_softmax(_dot_nt(qc.astype(_BF), kbuf[cpt]) * ATT_SCALE))
            a = (ps[0] - lam * ps[1]).astype(_BF)
            o = _dot(a, vbuf[...])
            o_ref[rows, base:base + w] = (_rms_lanes(o, sub_ref[...]) * (1.0 - lam_init)).astype(o_ref.dtype)


def _diff_attn(proj, lam_p, subln, lam_init, mix, new_k, new_v, *, row0, nb, L, layer,
               cache_k=None, cache_v=None, rope=None):
    latent = cache_k is not None
    rb0 = row0 // L
    w = 2 * HEAD_DIM
    nkeys = L + (PAST_LEN if latent else 0)
    cw = w if latent else GROUP_W

    def col(off):
        return pl.BlockSpec((L, cw), lambda b, h, off=off: (rb0 + b, off // cw + h))

    in_specs = [col(OFF_QA), col(OFF_KA), col(OFF_VA)]
    args = [proj, proj, proj]
    if latent:
        cspec = pl.BlockSpec((None, None, PAST_LEN, w), lambda b, h: (b, layer, 0, h))
        tspec = pl.BlockSpec((L, HEAD_DIM), lambda b, h: (0, 0))
        in_specs += [cspec, cspec, tspec, tspec]
        args += [cache_k.reshape(nb, DEPTH, PAST_LEN, GROUP_W), cache_v.reshape(nb, DEPTH, PAST_LEN, GROUP_W), rope[0], rope[1]]
    in_specs += [pl.BlockSpec((4, HEAD_DIM), lambda b, h: (0, 0)), pl.BlockSpec((1, w), lambda b, h: (0, 0))]
    args += [lam_p, subln.reshape(1, w)]
    out_shape = [MIX_SDS]
    out_specs = [pl.BlockSpec((L, cw), lambda b, h: (rb0 + b, MIX_COL["a"] // cw + h))]
    bufs = [mix]
    if not latent:
        kv_sds = jax.ShapeDtypeStruct((nb, DEPTH, L, A_HEADS, w), _F32)
        kv_spec = pl.BlockSpec((None, None, L, A_HEADS, w), lambda b, h: (b, layer, 0, 0, 0))
        out_shape += [kv_sds, kv_sds]
        out_specs += [kv_spec, kv_spec]
        bufs += [new_k, new_v]
    res = _call_into(
        functools.partial(_diff_kernel, L=L, latent=latent, lam_init=lam_init), bufs,
        grid=(nb, A_HEADS if latent else 1),
        in_specs=in_specs,
        args=args,
        out_specs=out_specs,
        out_shape=out_shape,
        scratch_shapes=[pltpu.VMEM((2, nkeys, HEAD_DIM), _BF), pltpu.VMEM((nkeys, w), _BF)],
        sem=("parallel", "parallel"),
        name="diff_attn_lat" if latent else "diff_attn_ctx",
    )
    return (res[0], new_k, new_v) if latent else tuple(res)


def _gqa_kernel(*refs, L, latent):
    it = iter(refs)
    q_ref, k_ref, v_ref = next(it), next(it), next(it)
    if latent:
        ck_ref, cv_ref, cos_ref, sin_ref = next(it), next(it), next(it), next(it)
    qn_ref, kn_ref, o_ref = next(it), next(it), next(it)
    if not latent:
        dk_ref, dv_ref = next(it), next(it)
    kbuf, vbuf = next(it), next(it)
    wq = D_GROUPS * HEAD_DIM

    for n in range(1 if latent else D_KV_HEADS):
        ks = slice(n * HEAD_DIM, (n + 1) * HEAD_DIM)
        kn = _rms_lanes(k_ref[:, ks], kn_ref[...])
        vv = v_ref[:, ks]
        if latent:
            kn = _rope(kn, cos_ref[...], sin_ref[...])
            kbuf[L:, :] = ck_ref[...].astype(_BF)
            vbuf[L:, :] = cv_ref[...].astype(_BF)
        else:
            dk_ref[:, n, :] = kn
            dv_ref[:, n, :] = vv
        kbuf[0:L, :] = kn.astype(_BF)
        vbuf[0:L, :] = vv.astype(_BF)

        for g in range(D_GROUPS):
            cs = slice(n * wq + g * HEAD_DIM, n * wq + (g + 1) * HEAD_DIM)
            for qi in range(L // ATT_TQ):
                rows = slice(qi * ATT_TQ, (qi + 1) * ATT_TQ)
                qg = _rms_lanes(q_ref[rows, cs], qn_ref[...])
                if latent:
                    qg = _rope(qg, cos_ref[rows, :], sin_ref[rows, :])
                p = _softmax(_dot_nt(qg.astype(_BF), kbuf[...]) * ATT_SCALE)
                o_ref[rows, cs] = _dot(p.astype(_BF), vbuf[...]).astype(o_ref.dtype)


def _gqa(proj, qnorm, knorm, mix, new_k, new_v, *, row0, nb, L, layer, cache_k=None, cache_v=None, rope=None):
    latent = cache_k is not None
    rb0 = row0 // L
    nh = 1 if latent else D_KV_HEADS
    wq = nh * D_GROUPS * HEAD_DIM
    wk = nh * HEAD_DIM
    nkeys = L + (PAST_LEN if latent else 0)
    in_specs = [pl.BlockSpec((L, wq), lambda b, n: (rb0 + b, OFF_QD // wq + n)),
                pl.BlockSpec((L, wk), lambda b, n: (rb0 + b, OFF_KD // wk + n)),
                pl.BlockSpec((L, wk), lambda b, n: (rb0 + b, OFF_VD // wk + n))]
    args = [proj, proj, proj]
    if latent:
        cspec = pl.BlockSpec((None, None, PAST_LEN, HEAD_DIM), lambda b, n: (b, layer, 0, n))
        tspec = pl.BlockSpec((L, HEAD_DIM), lambda b, n: (0, 0))
        in_specs += [cspec, cspec, tspec, tspec]
        kvw = D_KV_HEADS * HEAD_DIM
        args += [cache_k.reshape(nb, DEPTH, PAST_LEN, kvw), cache_v.reshape(nb, DEPTH, PAST_LEN, kvw), rope[0], rope[1]]
    gspec = pl.BlockSpec((1, HEAD_DIM), lambda b, n: (0, 0))
    in_specs += [gspec, gspec]
    args += [qnorm.reshape(1, HEAD_DIM), knorm.reshape(1, HEAD_DIM)]
    out_shape = [MIX_SDS]
    out_specs = [pl.BlockSpec((L, wq), lambda b, n: (rb0 + b, MIX_COL["d"] // wq + n))]
    bufs = [mix]
    if not latent:
        kv_sds = jax.ShapeDtypeStruct((nb, DEPTH, L, D_KV_HEADS, HEAD_DIM), _F32)
        kv_spec = pl.BlockSpec((None, None, L, D_KV_HEADS, HEAD_DIM), lambda b, n: (b, layer, 0, 0, 0))
        out_shape += [kv_sds, kv_sds]
        out_specs += [kv_spec, kv_spec]
        bufs += [new_k, new_v]
    res = _call_into(
        functools.partial(_gqa_kernel, L=L, latent=latent), bufs,
        grid=(nb, D_KV_HEADS // nh),
        in_specs=in_specs,
        args=args,
        out_specs=out_specs,
        out_shape=out_shape,
        scratch_shapes=[pltpu.VMEM((nkeys, HEAD_DIM), _BF), pltpu.VMEM((nkeys, HEAD_DIM), _BF)],
        sem=("parallel", "parallel"),
        name="gqa_lat" if latent else "gqa_ctx",
    )
    return (res[0], new_k, new_v) if latent else tuple(res)


HY_CB = 256


def _dft_mats(L):
    f = np.arange(L, dtype=np.float64)[:, None] + 0.5
    t = np.arange(L, dtype=np.float64)[None, :]
    ang = 2.0 * np.pi * f * t / (2 * L)
    out = []
    for m in (np.cos(ang), np.sin(ang), np.cos(ang).T, np.sin(ang).T):
        m32 = jnp.asarray(m.astype(np.float32))
        hi = m32.astype(_BF)
        out += [hi, (m32 - hi.astype(_F32)).astype(_BF)]
    return tuple(out)


def _split_bf16(x):
    hi = x.astype(_BF)
    return hi, (x - hi.astype(_F32)).astype(_BF)


def _dot3(m_hi, m_lo, x):
    xh, xl = _split_bf16(x)
    return _dot(m_hi, xh) + (_dot(m_hi, xl) + _dot(m_lo, xh))


def _filter_features(L):
    t = jnp.arange(L, dtype=_F32) / L
    bands = jnp.arange(1, HY_NFREQ + 1, dtype=_F32)
    ang = 2.0 * math.pi * t[:, None] * bands
    z = jnp.concatenate([t[:, None], jnp.cos(ang), jnp.sin(ang)], axis=-1)
    return z, t[:, None]


def _filt_kernel(z_ref, t_ref, w1_ref, b1_ref, w2_ref, b2_ref, w3f_ref, w3b_ref, fr_ref, df_ref, db_ref, sk_ref,
                 cmh_ref, cml_ref, smh_ref, sml_ref, hr_ref, hi_ref, *, L):
    fr = fr_ref[...]
    hid = jnp.sin(fr * (_dot(z_ref[...], w1_ref[...], _HI) + b1_ref[...]))
    hid = jnp.sin(fr * (_dot(hid, w2_ref[...], _HI) + b2_ref[...]))
    t = t_ref[...]

    def filt(w3_ref, d_ref):
        h = _dot(hid, w3_ref[...], _HI) * (jnp.exp(-t * jnp.abs(d_ref[...])) + HY_SHIFT)
        return h / (jnp.sum(jnp.abs(h), axis=0, keepdims=True) + EPS)

    hf = filt(w3f_ref, df_ref)
    hb = filt(w3b_ref, db_ref)
    hr_ref[...] = (_dot3(cmh_ref[...], cml_ref[...], hf + hb) + sk_ref[...]) * (1.0 / L)
    hi_ref[...] = _dot3(smh_ref[...], sml_ref[...], hb - hf) * (1.0 / L)


def _const_spec(shape):
    zeros = (0,) * len(shape)
    return pl.BlockSpec(shape, lambda *_: zeros, pipeline_mode=pl.Buffered(1))


def _hyena_filters(L, mats, w1, b1, w2, b2, w3, freq, delta, skip):
    z, t = _filter_features(L)
    cb = HY_CB
    ncb = B_W // cb
    full = lambda shape: pl.BlockSpec(shape, lambda l, o, j: (0,) * len(shape))
    per_layer = lambda shape: pl.BlockSpec((None,) + shape, lambda l, o, j: (l,) + (0,) * len(shape))
    mspec = _const_spec((L, L))
    in_specs = [full((L, HY_EMB)), full((L, 1)),
                per_layer((HY_EMB, HY_FFN)), per_layer((1, HY_FFN)),
                per_layer((HY_FFN, HY_FFN)), per_layer((1, HY_FFN)),
                pl.BlockSpec((None, HY_FFN, cb), lambda l, o, j: (l, 0, o * ncb + j)),
                pl.BlockSpec((None, HY_FFN, cb), lambda l, o, j: (l, 0, (2 + o) * ncb + j)),
                per_layer((1, HY_FFN)),
                pl.BlockSpec((None, None, 1, cb), lambda l, o, j: (l, o, 0, j)),
                pl.BlockSpec((None, None, 1, cb), lambda l, o, j: (l, 2 + o, 0, j)),
                pl.BlockSpec((None, None, 1, cb), lambda l, o, j: (l, o, 0, j)),
                mspec, mspec, mspec, mspec]
    out_spec = pl.BlockSpec((None, None, L, cb), lambda l, o, j: (l, o, 0, j))
    out_sds = jax.ShapeDtypeStruct((DEPTH, 2, L, B_W), _F32)
    return pl.pallas_call(
        functools.partial(_filt_kernel, L=L),
        grid=(DEPTH, 2, ncb),
        in_specs=in_specs,
        out_specs=[out_spec, out_spec],
        out_shape=[out_sds, out_sds],
        compiler_params=_cparams(("parallel", "parallel", "parallel")),
        name=f"hyena_filters_{L}",
    )(z, t, w1, b1.reshape(DEPTH, 1, HY_FFN), w2, b2.reshape(DEPTH, 1, HY_FFN), w3, w3,
      freq.reshape(DEPTH, 1, HY_FFN), delta.reshape(DEPTH, 4, 1, B_W), delta.reshape(DEPTH, 4, 1, B_W),
      skip.reshape(DEPTH, 2, 1, B_W), *mats[:4])


def _short_conv(u, w, b):
    L = u.shape[0]
    row = lax.broadcasted_iota(jnp.int32, u.shape, 0)
    prev = jnp.where(row == 0, 0.0, pltpu.roll(u, 1, 0))
    nxt = jnp.where(row == L - 1, 0.0, pltpu.roll(u, L - 1, 0))
    return prev * w[0:1, :] + u * w[1:2, :] + nxt * w[2:3, :] + b


def _hyena_kernel(v_ref, x1_ref, x2_ref, wv_ref, w1_ref, w2_ref, bv_ref, b1_ref, b2_ref, hr_ref, hi_ref,
                  cm_ref, sm_ref, cmt_ref, smt_ref, o_ref):
    def conv(u, o):
        ub = u.astype(_BF)
        a = _dot(cm_ref[...], ub)
        b = _dot(sm_ref[...], ub)
        hr = hr_ref[o]
        hi = hi_ref[o]
        return (_dot(cmt_ref[...], (a * hr + b * hi).astype(_BF))
                - _dot(smt_ref[...], (a * hi - b * hr).astype(_BF)))

    v = _short_conv(v_ref[...], wv_ref[...], bv_ref[...])
    z = _short_conv(x1_ref[...], w1_ref[...], b1_ref[...]) * conv(v, 0)
    y = _short_conv(x2_ref[...], w2_ref[...], b2_ref[...]) * conv(z, 1)
    o_ref[...] = y.astype(o_ref.dtype)


def _hyena(proj, conv_w, conv_b, hr, hi, mats, mix, *, row0, nb, L, layer):
    rb0 = row0 // L
    cb = min(B_W, HY_CB * DEC_SEQ // L)
    ncb = B_W // cb

    def ucol(part):
        return pl.BlockSpec((L, cb), lambda b, j, part=part: (rb0 + b, OFF_UB // cb + part * ncb + j))

    def wcol(rows, part):
        return pl.BlockSpec((rows, cb), lambda b, j, part=part: (0, part * ncb + j))

    hspec = pl.BlockSpec((None, 2, L, cb), lambda b, j: (layer, 0, 0, j))
    mspec = _const_spec((L, L))
    return _call_into(
        _hyena_kernel, [mix],
        grid=(nb, ncb),
        in_specs=[ucol(0), ucol(1), ucol(2), wcol(3, 0), wcol(3, 1), wcol(3, 2), wcol(1, 0), wcol(1, 1), wcol(1, 2),
                  hspec, hspec] + [mspec] * 4,
        args=[proj, proj, proj, conv_w, conv_w, conv_w, conv_b, conv_b, conv_b, hr, hi, *mats[0::2]],
        out_specs=[pl.BlockSpec((L, cb), lambda b, j: (rb0 + b, MIX_COL["b"] // cb + j))],
        out_shape=[MIX_SDS],
        scratch_shapes=[],
        sem=("parallel", "parallel"),
        name=f"hyena_{L}",
    )[0]


def _mixers(proj, l, lam_init, lp, mix, outs, *, row0, nb, L, caches, rope, filt, mats):
    kw = dict(row0=row0, nb=nb, L=L, layer=l)
    ak, av, dk, dv, st = outs
    if caches is not None:
        att = dict(rope=rope, **kw)
        mix, _, _ = _diff_attn(proj, lp["a_lam"], lp["a_subln"], lam_init, mix, ak, av,
                               cache_k=caches[0], cache_v=caches[1], **att)
        mix, _, _ = _gqa(proj, lp["d_qnorm"], lp["d_knorm"], mix, dk, dv, cache_k=caches[2], cache_v=caches[3], **att)
        mix, _ = _hgrn2(proj, lp["c_lb"], lp["c_gnorm"], caches[4], mix, st, **kw)
    else:
        mix, ak, av = _diff_attn(proj, lp["a_lam"], lp["a_subln"], lam_init, mix, ak, av, **kw)
        mix, dk, dv = _gqa(proj, lp["d_qnorm"], lp["d_knorm"], mix, dk, dv, **kw)
        mix, st = _hgrn2(proj, lp["c_lb"], lp["c_gnorm"], None, mix, st, **kw)
    mix = _hyena(proj, lp["b_conv_w"], lp["b_conv_b"], filt[0], filt[1], mats, mix, **kw)
    return mix, (ak, av, dk, dv, st)


def kernel(x_prompt, x_sample, c, cache_a_k, cache_a_v, cache_d_k, cache_d_v, state_c, c_ctx, w_mod, b_mod, g_norm1, g_norm2, w_in, w_out, a_lam, a_subln, b_conv_w, b_conv_b, b_ffn_w1, b_ffn_b1, b_ffn_w2, b_ffn_b2, b_ffn_w3, b_freq, b_delta, b_skip, c_lb_raw, c_gnorm, d_qnorm, d_knorm, w_mlp1, w_mlp2, g_final):
    p = jax.nn.softmax(c_lb_raw.astype(_F32), axis=0)
    lb_all = jnp.cumsum(p, axis=0) - p[:1]

    cvec = jnp.concatenate([c_ctx[None, :], c, jnp.zeros((MOD_ROWS - 1 - DEC_BATCH, D_MODEL), _F32)], axis=0)
    mods = _modulation(cvec, w_mod, b_mod)

    w_in_b, w_out_b = w_in.astype(_BF), w_out.astype(_BF)
    w_mlp1_b, w_mlp2_b = w_mlp1.astype(_BF), w_mlp2.astype(_BF)

    mats_c, mats_l = _dft_mats(SEQ), _dft_mats(DEC_SEQ)
    fargs = (b_ffn_w1, b_ffn_b1, b_ffn_w2, b_ffn_b2, b_ffn_w3, b_freq, b_delta, b_skip)
    filt_c = _hyena_filters(SEQ, mats_c, *fargs)
    filt_l = _hyena_filters(DEC_SEQ, mats_l, *fargs)
    rope = _rope_tables(DEC_SEQ)

    x = jnp.concatenate([x_prompt.reshape(N_CTX, D_MODEL), x_sample.reshape(N_LAT, D_MODEL)], axis=0)
    outs = (None,) * 5
    for l in range(DEPTH):
        lam_init = 0.8 - 0.6 * math.exp(-0.3 * l)
        lp = dict(a_lam=a_lam[l], a_subln=a_subln[l], b_conv_w=b_conv_w[l], b_conv_b=b_conv_b[l].reshape(1, 3 * B_W),
                  c_lb=lb_all[l], c_gnorm=c_gnorm[l].reshape(1, C_DV), d_qnorm=d_qnorm[l], d_knorm=d_knorm[l])
        h = _norm(x, g_norm1[l], mods[l], sc_idx=1, sh_idx=0)
        proj = _matmul(h, w_in_b, l, name="in_proj")
        mix, outs = _mixers(proj, l, lam_init, lp, None, outs, row0=0, nb=BATCH, L=SEQ, caches=None, rope=None,
                            filt=filt_c, mats=mats_c)
        mix, outs = _mixers(proj, l, lam_init, lp, mix, outs, row0=N_CTX, nb=DEC_BATCH, L=DEC_SEQ,
                            caches=(cache_a_k, cache_a_v, cache_d_k, cache_d_v, state_c), rope=rope,
                            filt=filt_l, mats=mats_l)
        x = _matmul(mix, w_out_b, l, name="out_proj", epilogue="resid", x=x, mods=mods[l], gt_idx=2)
        h2 = _norm(x, g_norm2[l], mods[l], sc_idx=4, sh_idx=3)
        hid = _matmul(h2, w_mlp1_b, l, name="mlp1", epilogue="relu2", out_dtype=_BF)
        x = _matmul(hid, w_mlp2_b, l, name="mlp2", epilogue="resid", x=x, mods=mods[l], gt_idx=5)

    y_prompt = _norm(x, g_final, row0=0, nrows=N_CTX).reshape(BATCH, SEQ, D_MODEL)
    y_sample = _norm(x, g_final, row0=N_CTX, nrows=N_LAT).reshape(DEC_BATCH, DEC_SEQ, D_MODEL)
    return (y_prompt, y_sample) + outs
```

```python
import functools
import math

import jax
import jax.numpy as jnp
import numpy as np
from jax import lax
from jax.experimental import pallas as pl
from jax.experimental.pallas import tpu as pltpu

D_MODEL = 4096
BATCH = 32
SEQ = 256
DEPTH = 4
DEC_BATCH = 4
DEC_SEQ = 1024
PAST_LEN = 512
GRID_W = 64
HEAD_DIM = 128
GROUP_W = D_MODEL // 4
A_HEADS = GROUP_W // (2 * HEAD_DIM)
B_W = GROUP_W
C_HEADS = 8
C_DK = 128
C_DV = GROUP_W // C_HEADS
C_W = C_HEADS * C_DK
D_HEADS = GROUP_W // HEAD_DIM
D_KV_HEADS = 2
D_GROUPS = D_HEADS // D_KV_HEADS
D_FF = 4 * D_MODEL
ROPE_BASE = 10000.0
ROPE_FREQS = HEAD_DIM // 4
HY_NFREQ = 16
HY_EMB = 1 + 2 * HY_NFREQ
HY_FFN = 64
HY_SHIFT = 0.05
EPS = 1e-6
GATE_FLOOR = 1e-20

N_CTX = BATCH * SEQ
N_LAT = DEC_BATCH * DEC_SEQ
N_TOK = N_CTX + N_LAT

OFF_QA = 0
OFF_KA = OFF_QA + GROUP_W
OFF_VA = OFF_KA + GROUP_W
OFF_UB = OFF_VA + GROUP_W
OFF_QC = OFF_UB + 3 * B_W
OFF_IC = OFF_QC + C_W
OFF_FF = OFF_IC + C_HEADS * C_DV
OFF_FB = OFF_FF + C_W
OFF_GZ = OFF_FB + C_W
OFF_QD = OFF_GZ + C_HEADS * C_DV
OFF_KD = OFF_QD + D_HEADS * HEAD_DIM
OFF_VD = OFF_KD + D_KV_HEADS * HEAD_DIM
IN_W = OFF_VD + D_KV_HEADS * HEAD_DIM

LANES = 128
SUBLANES = 8
VMEM_LIMIT = 56 * 1024 * 1024

CHUNK = 128

_HI = lax.Precision.HIGHEST
_BF = jnp.bfloat16
_F32 = jnp.float32


def _cparams(sem):
    return pltpu.CompilerParams(dimension_semantics=sem, vmem_limit_bytes=VMEM_LIMIT)


def _call_into(kernel_fn, bufs, *, grid, in_specs, args, out_specs, out_shape, scratch_shapes, sem, name):
    held = [b for b in bufs if b is not None]
    n = len(held)
    aliases = {}
    for oi, b in enumerate(bufs):
        if b is not None:
            aliases[len(aliases)] = oi

    def body(*refs):
        kernel_fn(*refs[n:])

    return pl.pallas_call(
        body,
        grid=grid,
        in_specs=[pl.BlockSpec(memory_space=pl.ANY)] * n + list(in_specs),
        out_specs=list(out_specs),
        out_shape=list(out_shape),
        scratch_shapes=list(scratch_shapes),
        input_output_aliases=aliases,
        compiler_params=_cparams(sem),
        name=name,
    )(*held, *args)


def _dot_nt(a, b):
    return lax.dot_general(a, b, (((1,), (1,)), ((), ())), preferred_element_type=_F32)


def _dot_tn(a, b):
    return lax.dot_general(a, b, (((0,), (0,)), ((), ())), preferred_element_type=_F32)


def _dot(a, b, precision=None):
    return jnp.dot(a, b, preferred_element_type=_F32, precision=precision)


HGRN_LEVELS = tuple(2 ** i for i in range(1, CHUNK.bit_length()))
LOG2E = 1.4426950408889634


def _hgrn_consts():
    C = CHUNK
    t = np.arange(C)[:, None]
    s = np.arange(C)[None, :]
    sgn = np.zeros((2, len(HGRN_LEVELS), C, LANES), np.float32)
    mask = np.zeros((2, len(HGRN_LEVELS) + 1, C, C), np.float32)
    tri = np.zeros((2, C, C), np.float32)
    for d in range(2):
        for li, blk in enumerate(HGRN_LEVELS):
            half = blk // 2
            q_t = ((t % blk) >= half) if d == 0 else ((t % blk) < half)
            q_s = ((s % blk) >= half) if d == 0 else ((s % blk) < half)
            sgn[d, li] = np.where(q_t, 1.0, -1.0)
            mask[d, li] = ((t // blk) == (s // blk)) & q_t & ~q_s
        mask[d, -1] = (t == s)
        tri[d] = (s <= t) if d == 0 else (s >= t)
    return jnp.asarray(tri, _BF), jnp.asarray(sgn), jnp.asarray(mask)


def _ref_rows(G, blk, reverse):
    C = G.shape[0]
    half = blk // 2
    rr = half if reverse else half - 1
    if blk >= SUBLANES:
        GB = G.reshape(C // blk, blk, LANES)
        return jnp.broadcast_to(GB[:, rr:rr + 1, :], GB.shape).reshape(C, LANES)
    if blk == 4:
        G3 = G.reshape(C // SUBLANES, SUBLANES, LANES)
        sub = lax.broadcasted_iota(jnp.int32, G3.shape, 1)
        lo = jnp.broadcast_to(G3[:, rr:rr + 1, :], G3.shape)
        hi = jnp.broadcast_to(G3[:, rr + 4:rr + 5, :], G3.shape)
        return jnp.where(sub < 4, lo, hi).reshape(C, LANES)
    odd = (lax.broadcasted_iota(jnp.int32, G.shape, 0) & 1) == 1
    if reverse:
        return jnp.where(odd, G, pltpu.roll(G, C - 1, 0))
    return jnp.where(odd, pltpu.roll(G, 1, 0), G)


def _hgrn_intra(qq, kk, vb, lf, d, tri_ref, sgn_ref, mask_ref):
    C = CHUNK
    l1 = lf.astype(_BF)
    r1 = lf - l1.astype(_F32)
    l2 = r1.astype(_BF)
    l3 = (r1 - l2.astype(_F32)).astype(_BF)
    g3 = _dot(tri_ref[d], jnp.concatenate([l1, l2, l3], axis=1))
    G2 = (g3[:, 0:LANES] + (g3[:, LANES:2 * LANES] + g3[:, 2 * LANES:3 * LANES])) * LOG2E
    scores = mask_ref[d, len(HGRN_LEVELS)] * _dot_nt(qq.astype(_BF), kk.astype(_BF))
    for li, blk in enumerate(HGRN_LEVELS):
        sg = sgn_ref[d, li]
        e = jnp.exp2((G2 - _ref_rows(G2, blk, d == 1)) * sg)
        x = (jnp.where(sg > 0.0, qq, kk) * e).astype(_BF)
        scores = scores + mask_ref[d, li] * _dot_nt(x, x)
    o = _dot(scores.astype(_BF), vb)
    qdec = (qq * jnp.exp2(G2)).astype(_BF)
    gend = G2[0:1, :] if d == 1 else G2[C - 1:C, :]
    kdec = (kk * jnp.exp2(gend - G2)).astype(_BF)
    return o, qdec, _dot_tn(vb, kdec), jnp.exp2(gend)


def _chunk_loop(n, body):
    if n <= 2:
        for c in range(n):
            body(c, 0)
    else:
        lax.fori_loop(0, n, body, 0, unroll=2)


def _hgrn2_kernel(*refs, L, has_state, emit_state):
    it = iter(refs)
    q_ref, v_ref, zf_ref, zb_ref, gz_ref, lb_ref, gn_ref, tri_ref, sgn_ref, mask_ref = (next(it) for _ in range(10))
    s0_ref = next(it) if has_state else None
    o_ref = next(it)
    sout_ref = next(it) if emit_state else None
    oacc, qdec_buf, ubuf, dbuf, st_ref = (next(it) for _ in range(5))
    nchunk = L // CHUNK

    def rows_of(c):
        start = c * CHUNK
        return pl.ds(start if isinstance(c, int) else pl.multiple_of(start, CHUNK), CHUNK)

    def local(c, carry):
        rows = rows_of(c)
        qq = q_ref[rows, :]
        qq = qq * jax.nn.sigmoid(qq)
        vb = v_ref[rows, :].astype(_BF)
        for d, z_ref in enumerate((zf_ref, zb_ref)):
            z = z_ref[rows, :]
            lb = lb_ref[d:d + 1, :]
            lf = jnp.log(jnp.maximum(lb + (1.0 - lb) * jax.nn.sigmoid(z), GATE_FLOOR))
            kk = (1.0 - lb) * jax.nn.sigmoid(-z)
            o, qd, u, dec = _hgrn_intra(qq, kk, vb, lf, d, tri_ref, sgn_ref, mask_ref)
            oacc[d, rows, :] = o
            qdec_buf[d, rows, :] = qd
            ubuf[d, c] = u
            dbuf[d, c] = dec
        return carry

    _chunk_loop(nchunk, local)

    for d in range(2):
        st_ref[d] = s0_ref[d].T if has_state else jnp.zeros((C_DV, C_DK), _F32)

    def scan(i, carry):
        for d in range(2):
            c = (nchunk - 1 - i) if d == 1 else i
            rows = rows_of(c)
            st = st_ref[d]
            oacc[d, rows, :] = oacc[d, rows, :] + _dot_nt(qdec_buf[d, rows, :], st.astype(_BF))
            st_ref[d] = st * dbuf[d, c] + ubuf[d, c]
        return carry

    _chunk_loop(nchunk, scan)
    if emit_state:
        for d in range(2):
            sout_ref[d] = st_ref[d].T

    def finish(c, carry):
        rows = rows_of(c)
        tot = oacc[0, rows, :] + oacc[1, rows, :]
        g = gz_ref[rows, :]
        o_ref[rows, :] = (_rms_lanes(tot, gn_ref[...]) * (g * jax.nn.sigmoid(g))).astype(o_ref.dtype)
        return carry

    _chunk_loop(nchunk, finish)


MIX_COL = dict(a=0, b=GROUP_W, c=2 * GROUP_W, d=3 * GROUP_W)
MIX_SDS = jax.ShapeDtypeStruct((N_TOK, D_MODEL), _BF)


def _hgrn2(proj, lb, gnorm, s0, mix, states, *, row0, nb, L, layer):
    has_state = s0 is not None
    emit_state = not has_state
    rb0 = row0 // L
    nchunk = L // CHUNK
    tri, sgn, mask = _hgrn_consts()

    def col(off):
        return pl.BlockSpec((L, LANES), lambda b, h, off=off: (rb0 + b, off // LANES + h))

    in_specs = [col(OFF_QC), col(OFF_IC), col(OFF_FF), col(OFF_FB), col(OFF_GZ),
                pl.BlockSpec((2, LANES), lambda b, h: (0, h)),
                pl.BlockSpec((1, C_DV), lambda b, h: (0, 0)),
                _const_spec(tri.shape), _const_spec(sgn.shape), _const_spec(mask.shape)]
    args = [proj, proj, proj, proj, proj, lb, gnorm, tri, sgn, mask]
    if has_state:
        in_specs.append(pl.BlockSpec((None, None, 2, None, C_DK, C_DV), lambda b, h: (b, layer, 0, h, 0, 0)))
        args.append(s0)
    out_shape = [MIX_SDS]
    out_specs = [pl.BlockSpec((L, LANES), lambda b, h: (rb0 + b, MIX_COL["c"] // LANES + h))]
    bufs = [mix]
    if emit_state:
        out_shape.append(jax.ShapeDtypeStruct((nb, DEPTH, 2, C_HEADS, C_DK, C_DV), _F32))
        out_specs.append(pl.BlockSpec((None, None, 2, None, C_DK, C_DV), lambda b, h: (b, layer, 0, h, 0, 0)))
        bufs.append(states)
    res = _call_into(
        functools.partial(_hgrn2_kernel, L=L, has_state=has_state, emit_state=emit_state), bufs,
        grid=(nb, C_HEADS),
        in_specs=in_specs,
        args=args,
        out_specs=out_specs,
        out_shape=out_shape,
        scratch_shapes=[pltpu.VMEM((2, L, C_DV), _F32), pltpu.VMEM((2, L, C_DK), _BF),
                        pltpu.VMEM((2, nchunk, C_DV, C_DK), _F32), pltpu.VMEM((2, nchunk, 1, C_DK), _F32),
                        pltpu.VMEM((2, C_DV, C_DK), _F32)],
        sem=("parallel", "parallel"),
        name="hgrn2_lat" if has_state else "hgrn2_ctx",
    )
    return (res[0], res[1]) if emit_state else (res[0], states)


MOD_ROWS = 8
MOD_TN = 512


def _mod_row(i, tm):
    start = i * tm
    return jnp.where(start < N_CTX, 0, 1 + (start - N_CTX) // DEC_SEQ)


def _mod_kernel(c_ref, w_ref, b_ref, o_ref):
    cv = c_ref[...]
    s = (cv * jax.nn.sigmoid(cv)).astype(_BF)
    o_ref[...] = _dot(s, w_ref[...].astype(_BF)) + b_ref[...]


def _modulation(cvec, w_mod, b_mod):
    n = 6 * D_MODEL
    out = pl.pallas_call(
        _mod_kernel,
        grid=(DEPTH, n // MOD_TN),
        in_specs=[pl.BlockSpec((MOD_ROWS, D_MODEL), lambda l, j: (0, 0)),
                  pl.BlockSpec((None, D_MODEL, MOD_TN), lambda l, j: (l, 0, j)),
                  pl.BlockSpec((None, 1, MOD_TN), lambda l, j: (l, 0, j))],
        out_specs=pl.BlockSpec((None, MOD_ROWS, MOD_TN), lambda l, j: (l, 0, j)),
        out_shape=jax.ShapeDtypeStruct((DEPTH, MOD_ROWS, n), _F32),
        compiler_params=_cparams(("parallel", "parallel")),
        name="modulation",
    )(cvec, w_mod, b_mod.reshape(DEPTH, 1, n))
    return out.reshape(DEPTH, MOD_ROWS, 6, 1, D_MODEL).transpose(0, 2, 1, 3, 4)


NORM_TM = 512


def _norm_kernel(x_ref, g_ref, *rest, modulate):
    x = x_ref[...]
    y = x * lax.rsqrt(jnp.mean(x * x, axis=-1, keepdims=True) + EPS) * g_ref[...]
    if modulate:
        sc_ref, sh_ref, o_ref = rest
        y = y * (1.0 + sc_ref[...]) + sh_ref[...]
    else:
        (o_ref,) = rest
    o_ref[...] = y.astype(o_ref.dtype)


def _norm(x, g, mods=None, sc_idx=0, sh_idx=0, row0=0, nrows=N_TOK):
    modulate = mods is not None
    tm = NORM_TM
    t0 = row0 // tm
    in_specs = [pl.BlockSpec((tm, D_MODEL), lambda i: (t0 + i, 0)),
                pl.BlockSpec((1, D_MODEL), lambda i: (0, 0))]
    args = [x, g.reshape(1, D_MODEL)]
    if modulate:
        in_specs += [pl.BlockSpec((None, None, 1, D_MODEL), lambda i: (sc_idx, _mod_row(t0 + i, tm), 0, 0)),
                     pl.BlockSpec((None, None, 1, D_MODEL), lambda i: (sh_idx, _mod_row(t0 + i, tm), 0, 0))]
        args += [mods, mods]
    return pl.pallas_call(
        functools.partial(_norm_kernel, modulate=modulate),
        grid=(nrows // tm,),
        in_specs=in_specs,
        out_specs=pl.BlockSpec((tm, D_MODEL), lambda i: (i, 0)),
        out_shape=jax.ShapeDtypeStruct((nrows, D_MODEL), _BF if modulate else _F32),
        compiler_params=_cparams(("parallel",)),
        name="norm_mod" if modulate else "norm_final",
    )(*args)


def _mm_kernel(a_ref, b_ref, *rest, epilogue, nk, grid, side_chunks):
    ns = len(side_chunks)
    if epilogue == "resid":
        x_ref, gt_ref = rest[:2]
        rest = rest[2:]
    src_refs, o_ref, dst_refs, rest = rest[:ns], rest[ns], rest[ns + 1:2 * ns + 1], rest[2 * ns + 1:]

    step = (pl.program_id(0) * grid[1] + pl.program_id(1)) * grid[2] + pl.program_id(2)
    for src_ref, dst_ref, nchunks in zip(src_refs, dst_refs, side_chunks):
        @pl.when(step < nchunks)
        def _(src_ref=src_ref, dst_ref=dst_ref):
            dst_ref[...] = src_ref[...].astype(dst_ref.dtype)

    def finish(acc):
        if epilogue == "resid":
            o_ref[...] = x_ref[...] + gt_ref[...] * acc
        elif epilogue == "relu2":
            o_ref[...] = jnp.square(jnp.maximum(acc, 0.0)).astype(o_ref.dtype)
        else:
            o_ref[...] = acc.astype(o_ref.dtype)

    if nk == 1:
        finish(_dot(a_ref[...], b_ref[...]))
        return
    (acc_ref,) = rest
    k = pl.program_id(2)

    @pl.when(k == 0)
    def _():
        acc_ref[...] = jnp.zeros_like(acc_ref)

    acc_ref[...] += _dot(a_ref[...], b_ref[...])

    @pl.when(k == nk - 1)
    def _():
        finish(acc_ref[...])


MM_TILES = dict(in_proj=(1024, 1280, D_MODEL), out_proj=(1024, 1024, D_MODEL),
                mlp1=(1024, 1024, D_MODEL), mlp2=(1024, 1024, 2048))


SIDE_CHUNKS = 128


def _matmul(a, b, *, name, epilogue="plain", out_dtype=_F32, x=None, mods=None, gt_idx=0, side=()):
    M, K = a.shape
    N = b.shape[1]
    tm, tn, tk = MM_TILES[name]
    nk = K // tk
    grid = (M // tm, N // tn, nk)
    assert not side or grid[0] * grid[1] * grid[2] >= SIDE_CHUNKS
    in_specs = [pl.BlockSpec((tm, tk), lambda i, j, k: (i, k)),
                pl.BlockSpec((tk, tn), lambda i, j, k: (k, j))]
    args = [a, b]
    if epilogue == "resid":
        in_specs += [pl.BlockSpec((tm, tn), lambda i, j, k: (i, j)),
                     pl.BlockSpec((None, None, 1, tn), lambda i, j, k: (gt_idx, _mod_row(i, tm), 0, j))]
        args += [x, mods]

    def chunk(i, j, k):
        return jnp.minimum((i * grid[1] + j) * grid[2] + k, SIDE_CHUNKS - 1)

    out_specs = [pl.BlockSpec((tm, tn), lambda i, j, k: (i, j))]
    out_shape = [jax.ShapeDtypeStruct((M, N), out_dtype)]
    for w, layer in side:
        _, R, C = w.shape
        rows = R // SIDE_CHUNKS
        in_specs.append(pl.BlockSpec((None, rows, C), lambda i, j, k, layer=layer: (layer, chunk(i, j, k), 0)))
        args.append(w)
        out_specs.append(pl.BlockSpec((rows, C), lambda i, j, k: (chunk(i, j, k), 0)))
        out_shape.append(jax.ShapeDtypeStruct((R, C), _BF))
    return pl.pallas_call(
        functools.partial(_mm_kernel, epilogue=epilogue, nk=nk, grid=grid, side_chunks=(SIDE_CHUNKS,) * len(side)),
        grid=grid,
        in_specs=in_specs,
        out_specs=out_specs,
        out_shape=out_shape,
        scratch_shapes=[pltpu.VMEM((tm, tn), _F32)] if nk > 1 else [],
        compiler_params=_cparams(("arbitrary", "arbitrary", "arbitrary")),
        name=name,
    )(*args)


ATT_TQ = 256
ATT_SCALE = HEAD_DIM ** -0.5


def _rope(x, cos, sin):
    lane = lax.broadcasted_iota(jnp.int32, x.shape, 1)
    first = (lane % (2 * ROPE_FREQS)) < ROPE_FREQS
    partner = jnp.where(first, pltpu.roll(x, HEAD_DIM - ROPE_FREQS, 1), pltpu.roll(x, ROPE_FREQS, 1))
    return x * cos + partner * sin


def _rope_tables(L):
    pos = jnp.arange(L)
    rows = (pos // GRID_W).astype(_F32)
    cols = (pos % GRID_W).astype(_F32)
    inv = ROPE_BASE ** (-jnp.arange(ROPE_FREQS, dtype=_F32) / ROPE_FREQS)
    ang_r = rows[:, None] * inv
    ang_c = cols[:, None] * inv
    cos = jnp.concatenate([jnp.cos(ang_r)] * 2 + [jnp.cos(ang_c)] * 2, axis=-1)
    sin = jnp.concatenate([-jnp.sin(ang_r), jnp.sin(ang_r), -jnp.sin(ang_c), jnp.sin(ang_c)], axis=-1)
    return cos, sin


def _softmax(s):
    m = jnp.max(s, axis=-1, keepdims=True)
    e = jnp.exp(s - m)
    return e * (1.0 / jnp.sum(e, axis=-1, keepdims=True))


def _rms_lanes(x, g):
    return x * lax.rsqrt(jnp.mean(x * x, axis=-1, keepdims=True) + EPS) * g


def _diff_kernel(*refs, L, latent, lam_init):
    it = iter(refs)
    q_ref, k_ref, v_ref = next(it), next(it), next(it)
    if latent:
        ck_ref, cv_ref, cos_ref, sin_ref = next(it), next(it), next(it), next(it)
    lam_ref, sub_ref, o_ref = next(it), next(it), next(it)
    if not latent:
        ak_ref, av_ref = next(it), next(it)
    kbuf, vbuf = next(it), next(it)
    w = 2 * HEAD_DIM

    lp = lam_ref[...]
    lam = (jnp.exp(jnp.sum(lp[0:1] * lp[1:2], axis=-1, keepdims=True))
           - jnp.exp(jnp.sum(lp[2:3] * lp[3:4], axis=-1, keepdims=True)) + lam_init)
    for h in range(1 if latent else A_HEADS):
        base = h * w
        for cpt in range(2):
            cs = slice(base + cpt * HEAD_DIM, base + (cpt + 1) * HEAD_DIM)
            kc = k_ref[:, cs]
            if latent:
                kc = _rope(kc, cos_ref[...], sin_ref[...])
                kbuf[cpt, L:, :] = ck_ref[:, cs].astype(_BF)
            kbuf[cpt, 0:L, :] = kc.astype(_BF)
        vv = v_ref[:, base:base + w]
        vbuf[0:L, :] = vv.astype(_BF)
        if latent:
            vbuf[L:, :] = cv_ref[...].astype(_BF)
        else:
            ak_ref[:, h, :] = k_ref[:, base:base + w]
            av_ref[:, h, :] = vv

        for qi in range(L // ATT_TQ):
            rows = slice(qi * ATT_TQ, (qi + 1) * ATT_TQ)
            ps = []
            for cpt in range(2):
                qc = q_ref[rows, base + cpt * HEAD_DIM:base + (cpt + 1) * HEAD_DIM]
                if latent:
                    qc = _rope(qc, cos_ref[rows, :], sin_ref[rows, :])
                ps.append(_softmax(_dot_nt(qc.astype(_BF), kbuf[cpt]) * ATT_SCALE))
            a = (ps[0] - lam * ps[1]).astype(_BF)
            o = _dot(a, vbuf[...])
            o_ref[rows, base:base + w] = (_rms_lanes(o, sub_ref[...]) * (1.0 - lam_init)).astype(o_ref.dtype)


def _diff_attn(proj, lam_p, subln, lam_init, mix, new_k, new_v, *, row0, nb, L, layer,
               cache_k=None, cache_v=None, rope=None):
    latent = cache_k is not None
    rb0 = row0 // L
    w = 2 * HEAD_DIM
    nkeys = L + (PAST_LEN if latent else 0)
    cw = w if latent else GROUP_W

    def col(off):
        return pl.BlockSpec((L, cw), lambda b, h, off=off: (rb0 + b, off // cw + h))

    in_specs = [col(OFF_QA), col(OFF_KA), col(OFF_VA)]
    args = [proj, proj, proj]
    if latent:
        cspec = pl.BlockSpec((None, None, PAST_LEN, w), lambda b, h: (b, layer, 0, h))
        tspec = pl.BlockSpec((L, HEAD_DIM), lambda b, h: (0, 0))
        in_specs += [cspec, cspec, tspec, tspec]
        args += [cache_k.reshape(nb, DEPTH, PAST_LEN, GROUP_W), cache_v.reshape(nb, DEPTH, PAST_LEN, GROUP_W), rope[0], rope[1]]
    in_specs += [pl.BlockSpec((4, HEAD_DIM), lambda b, h: (0, 0)), pl.BlockSpec((1, w), lambda b, h: (0, 0))]
    args += [lam_p, subln.reshape(1, w)]
    out_shape = [MIX_SDS]
    out_specs = [pl.BlockSpec((L, cw), lambda b, h: (rb0 + b, MIX_COL["a"] // cw + h))]
    bufs = [mix]
    if not latent:
        kv_sds = jax.ShapeDtypeStruct((nb, DEPTH, L, A_HEADS, w), _F32)
        kv_spec = pl.BlockSpec((None, None, L, A_HEADS, w), lambda b, h: (b, layer, 0, 0, 0))
        out_shape += [kv_sds, kv_sds]
        out_specs += [kv_spec, kv_spec]
        bufs += [new_k, new_v]
    res = _call_into(
        functools.partial(_diff_kernel, L=L, latent=latent, lam_init=lam_init), bufs,
        grid=(nb, A_HEADS if latent else 1),
        in_specs=in_specs,
        args=args,
        out_specs=out_specs,
        out_shape=out_shape,
        scratch_shapes=[pltpu.VMEM((2, nkeys, HEAD_DIM), _BF), pltpu.VMEM((nkeys, w), _BF)],
        sem=("parallel", "parallel"),
        name="diff_attn_lat" if latent else "diff_attn_ctx",
    )
    return (res[0], new_k, new_v) if latent else tuple(res)


def _gqa_kernel(*refs, L, latent):
    it = iter(refs)
    q_ref, k_ref, v_ref = next(it), next(it), next(it)
    if latent:
        ck_ref, cv_ref, cos_ref, sin_ref = next(it), next(it), next(it), next(it)
    qn_ref, kn_ref, o_ref = next(it), next(it), next(it)
    if not latent:
        dk_ref, dv_ref = next(it), next(it)
    kbuf, vbuf = next(it), next(it)
    wq = D_GROUPS * HEAD_DIM

    for n in range(1 if latent else D_KV_HEADS):
        ks = slice(n * HEAD_DIM, (n + 1) * HEAD_DIM)
        kn = _rms_lanes(k_ref[:, ks], kn_ref[...])
        vv = v_ref[:, ks]
        if latent:
            kn = _rope(kn, cos_ref[...], sin_ref[...])
            kbuf[L:, :] = ck_ref[...].astype(_BF)
            vbuf[L:, :] = cv_ref[...].astype(_BF)
        else:
            dk_ref[:, n, :] = kn
            dv_ref[:, n, :] = vv
        kbuf[0:L, :] = kn.astype(_BF)
        vbuf[0:L, :] = vv.astype(_BF)

        for g in range(D_GROUPS):
            cs = slice(n * wq + g * HEAD_DIM, n * wq + (g + 1) * HEAD_DIM)
            for qi in range(L // ATT_TQ):
                rows = slice(qi * ATT_TQ, (qi + 1) * ATT_TQ)
                qg = _rms_lanes(q_ref[rows, cs], qn_ref[...])
                if latent:
                    qg = _rope(qg, cos_ref[rows, :], sin_ref[rows, :])
                p = _softmax(_dot_nt(qg.astype(_BF), kbuf[...]) * ATT_SCALE)
                o_ref[rows, cs] = _dot(p.astype(_BF), vbuf[...]).astype(o_ref.dtype)


def _gqa(proj, qnorm, knorm, mix, new_k, new_v, *, row0, nb, L, layer, cache_k=None, cache_v=None, rope=None):
    latent = cache_k is not None
    rb0 = row0 // L
    nh = 1 if latent else D_KV_HEADS
    wq = nh * D_GROUPS * HEAD_DIM
    wk = nh * HEAD_DIM
    nkeys = L + (PAST_LEN if latent else 0)
    in_specs = [pl.BlockSpec((L, wq), lambda b, n: (rb0 + b, OFF_QD // wq + n)),
                pl.BlockSpec((L, wk), lambda b, n: (rb0 + b, OFF_KD // wk + n)),
                pl.BlockSpec((L, wk), lambda b, n: (rb0 + b, OFF_VD // wk + n))]
    args = [proj, proj, proj]
    if latent:
        cspec = pl.BlockSpec((None, None, PAST_LEN, HEAD_DIM), lambda b, n: (b, layer, 0, n))
        tspec = pl.BlockSpec((L, HEAD_DIM), lambda b, n: (0, 0))
        in_specs += [cspec, cspec, tspec, tspec]
        kvw = D_KV_HEADS * HEAD_DIM
        args += [cache_k.reshape(nb, DEPTH, PAST_LEN, kvw), cache_v.reshape(nb, DEPTH, PAST_LEN, kvw), rope[0], rope[1]]
    gspec = pl.BlockSpec((1, HEAD_DIM), lambda b, n: (0, 0))
    in_specs += [gspec, gspec]
    args += [qnorm.reshape(1, HEAD_DIM), knorm.reshape(1, HEAD_DIM)]
    out_shape = [MIX_SDS]
    out_specs = [pl.BlockSpec((L, wq), lambda b, n: (rb0 + b, MIX_COL["d"] // wq + n))]
    bufs = [mix]
    if not latent:
        kv_sds = jax.ShapeDtypeStruct((nb, DEPTH, L, D_KV_HEADS, HEAD_DIM), _F32)
        kv_spec = pl.BlockSpec((None, None, L, D_KV_HEADS, HEAD_DIM), lambda b, n: (b, layer, 0, 0, 0))
        out_shape += [kv_sds, kv_sds]
        out_specs += [kv_spec, kv_spec]
        bufs += [new_k, new_v]
    res = _call_into(
        functools.partial(_gqa_kernel, L=L, latent=latent), bufs,
        grid=(nb, D_KV_HEADS // nh),
        in_specs=in_specs,
        args=args,
        out_specs=out_specs,
        out_shape=out_shape,
        scratch_shapes=[pltpu.VMEM((nkeys, HEAD_DIM), _BF), pltpu.VMEM((nkeys, HEAD_DIM), _BF)],
        sem=("parallel", "parallel"),
        name="gqa_lat" if latent else "gqa_ctx",
    )
    return (res[0], new_k, new_v) if latent else tuple(res)


HY_CB = 256


def _dft_mats(L):
    f = np.arange(L, dtype=np.float64)[:, None] + 0.5
    t = np.arange(L, dtype=np.float64)[None, :]
    ang = 2.0 * np.pi * f * t / (2 * L)
    out = []
    for m in (np.cos(ang), np.sin(ang), np.cos(ang).T, np.sin(ang).T):
        m32 = jnp.asarray(m.astype(np.float32))
        hi = m32.astype(_BF)
        out += [hi, (m32 - hi.astype(_F32)).astype(_BF)]
    return tuple(out)


def _split_bf16(x):
    hi = x.astype(_BF)
    return hi, (x - hi.astype(_F32)).astype(_BF)


def _dot3(m_hi, m_lo, x):
    xh, xl = _split_bf16(x)
    return _dot(m_hi, xh) + (_dot(m_hi, xl) + _dot(m_lo, xh))


def _filter_features(L):
    t = jnp.arange(L, dtype=_F32) / L
    bands = jnp.arange(1, HY_NFREQ + 1, dtype=_F32)
    ang = 2.0 * math.pi * t[:, None] * bands
    z = jnp.concatenate([t[:, None], jnp.cos(ang), jnp.sin(ang)], axis=-1)
    return z, t[:, None]


def _filt_kernel(z_ref, t_ref, w1_ref, b1_ref, w2_ref, b2_ref, w3f_ref, w3b_ref, fr_ref, df_ref, db_ref, sk_ref,
                 cmh_ref, cml_ref, smh_ref, sml_ref, hr_ref, hi_ref, *, L):
    fr = fr_ref[...]
    hid = jnp.sin(fr * (_dot(z_ref[...], w1_ref[...], _HI) + b1_ref[...]))
    hid = jnp.sin(fr * (_dot(hid, w2_ref[...], _HI) + b2_ref[...]))
    t = t_ref[...]

    def filt(w3_ref, d_ref):
        h = _dot(hid, w3_ref[...], _HI) * (jnp.exp(-t * jnp.abs(d_ref[...])) + HY_SHIFT)
        return h / (jnp.sum(jnp.abs(h), axis=0, keepdims=True) + EPS)

    hf = filt(w3f_ref, df_ref)
    hb = filt(w3b_ref, db_ref)
    hr_ref[...] = (_dot3(cmh_ref[...], cml_ref[...], hf + hb) + sk_ref[...]) * (1.0 / L)
    hi_ref[...] = _dot3(smh_ref[...], sml_ref[...], hb - hf) * (1.0 / L)


def _const_spec(shape):
    zeros = (0,) * len(shape)
    return pl.BlockSpec(shape, lambda *_: zeros, pipeline_mode=pl.Buffered(1))


def _hyena_filters(L, mats, w1, b1, w2, b2, w3, freq, delta, skip):
    z, t = _filter_features(L)
    cb = 2 * HY_CB
    ncb = B_W // cb
    full = lambda shape: pl.BlockSpec(shape, lambda l, o, j: (0,) * len(shape))
    per_layer = lambda shape: pl.BlockSpec((None,) + shape, lambda l, o, j: (l,) + (0,) * len(shape))
    mspec = _const_spec((L, L))
    in_specs = [full((L, HY_EMB)), full((L, 1)),
                per_layer((HY_EMB, HY_FFN)), per_layer((1, HY_FFN)),
                per_layer((HY_FFN, HY_FFN)), per_layer((1, HY_FFN)),
                pl.BlockSpec((None, HY_FFN, cb), lambda l, o, j: (l, 0, o * ncb + j)),
                pl.BlockSpec((None, HY_FFN, cb), lambda l, o, j: (l, 0, (2 + o) * ncb + j)),
                per_layer((1, HY_FFN)),
                pl.BlockSpec((None, None, 1, cb), lambda l, o, j: (l, o, 0, j)),
                pl.BlockSpec((None, None, 1, cb), lambda l, o, j: (l, 2 + o, 0, j)),
                pl.BlockSpec((None, None, 1, cb), lambda l, o, j: (l, o, 0, j)),
                mspec, mspec, mspec, mspec]
    out_spec = pl.BlockSpec((None, None, L, cb), lambda l, o, j: (l, o, 0, j))
    out_sds = jax.ShapeDtypeStruct((DEPTH, 2, L, B_W), _F32)
    return pl.pallas_call(
        functools.partial(_filt_kernel, L=L),
        grid=(DEPTH, 2, ncb),
        in_specs=in_specs,
        out_specs=[out_spec, out_spec],
        out_shape=[out_sds, out_sds],
        compiler_params=_cparams(("parallel", "parallel", "parallel")),
        name=f"hyena_filters_{L}",
    )(z, t, w1, b1.reshape(DEPTH, 1, HY_FFN), w2, b2.reshape(DEPTH, 1, HY_FFN), w3, w3,
      freq.reshape(DEPTH, 1, HY_FFN), delta.reshape(DEPTH, 4, 1, B_W), delta.reshape(DEPTH, 4, 1, B_W),
      skip.reshape(DEPTH, 2, 1, B_W), *mats[:4])


def _short_conv(u, w, b):
    L = u.shape[0]
    row = lax.broadcasted_iota(jnp.int32, u.shape, 0)
    prev = jnp.where(row == 0, 0.0, pltpu.roll(u, 1, 0))
    nxt = jnp.where(row == L - 1, 0.0, pltpu.roll(u, L - 1, 0))
    return prev * w[0:1, :] + u * w[1:2, :] + nxt * w[2:3, :] + b


def _hyena_kernel(v_ref, x1_ref, x2_ref, wv_ref, w1_ref, w2_ref, bv_ref, b1_ref, b2_ref, hr_ref, hi_ref,
                  cm_ref, sm_ref, cmt_ref, smt_ref, o_ref):
    def conv(u, o):
        ub = u.astype(_BF)
        a = _dot(cm_ref[...], ub)
        b = _dot(sm_ref[...], ub)
        hr = hr_ref[o]
        hi = hi_ref[o]
        return (_dot(cmt_ref[...], (a * hr + b * hi).astype(_BF))
                - _dot(smt_ref[...], (a * hi - b * hr).astype(_BF)))

    v = _short_conv(v_ref[...], wv_ref[...], bv_ref[...])
    z = _short_conv(x1_ref[...], w1_ref[...], b1_ref[...]) * conv(v, 0)
    y = _short_conv(x2_ref[...], w2_ref[...], b2_ref[...]) * conv(z, 1)
    o_ref[...] = y.astype(o_ref.dtype)


def _hyena(proj, conv_w, conv_b, hr, hi, mats, mix, *, row0, nb, L, layer):
    rb0 = row0 // L
    cb = min(B_W, HY_CB * DEC_SEQ // L)
    ncb = B_W // cb

    def ucol(part):
        return pl.BlockSpec((L, cb), lambda b, j, part=part: (rb0 + b, OFF_UB // cb + part * ncb + j))

    def wcol(rows, part):
        return pl.BlockSpec((rows, cb), lambda b, j, part=part: (0, part * ncb + j))

    hspec = pl.BlockSpec((None, 2, L, cb), lambda b, j: (layer, 0, 0, j))
    mspec = _const_spec((L, L))
    return _call_into(
        _hyena_kernel, [mix],
        grid=(nb, ncb),
        in_specs=[ucol(0), ucol(1), ucol(2), wcol(3, 0), wcol(3, 1), wcol(3, 2), wcol(1, 0), wcol(1, 1), wcol(1, 2),
                  hspec, hspec] + [mspec] * 4,
        args=[proj, proj, proj, conv_w, conv_w, conv_w, conv_b, conv_b, conv_b, hr, hi, *mats[0::2]],
        out_specs=[pl.BlockSpec((L, cb), lambda b, j: (rb0 + b, MIX_COL["b"] // cb + j))],
        out_shape=[MIX_SDS],
        scratch_shapes=[],
        sem=("parallel", "parallel"),
        name=f"hyena_{L}",
    )[0]


def _mixers(proj, l, lam_init, lp, mix, outs, *, row0, nb, L, caches, rope, filt, mats):
    kw = dict(row0=row0, nb=nb, L=L, layer=l)
    ak, av, dk, dv, st = outs
    if caches is not None:
        att = dict(rope=rope, **kw)
        mix, _, _ = _diff_attn(proj, lp["a_lam"], lp["a_subln"], lam_init, mix, ak, av,
                               cache_k=caches[0], cache_v=caches[1], **att)
        mix, _, _ = _gqa(proj, lp["d_qnorm"], lp["d_knorm"], mix, dk, dv, cache_k=caches[2], cache_v=caches[3], **att)
        mix, _ = _hgrn2(proj, lp["c_lb"], lp["c_gnorm"], caches[4], mix, st, **kw)
    else:
        mix, ak, av = _diff_attn(proj, lp["a_lam"], lp["a_subln"], lam_init, mix, ak, av, **kw)
        mix, dk, dv = _gqa(proj, lp["d_qnorm"], lp["d_knorm"], mix, dk, dv, **kw)
        mix, st = _hgrn2(proj, lp["c_lb"], lp["c_gnorm"], None, mix, st, **kw)
    mix = _hyena(proj, lp["b_conv_w"], lp["b_conv_b"], filt[0], filt[1], mats, mix, **kw)
    return mix, (ak, av, dk, dv, st)


def kernel(x_prompt, x_sample, c, cache_a_k, cache_a_v, cache_d_k, cache_d_v, state_c, c_ctx, w_mod, b_mod, g_norm1, g_norm2, w_in, w_out, a_lam, a_subln, b_conv_w, b_conv_b, b_ffn_w1, b_ffn_b1, b_ffn_w2, b_ffn_b2, b_ffn_w3, b_freq, b_delta, b_skip, c_lb_raw, c_gnorm, d_qnorm, d_knorm, w_mlp1, w_mlp2, g_final):
    p = jax.nn.softmax(c_lb_raw.astype(_F32), axis=0)
    lb_all = jnp.cumsum(p, axis=0) - p[:1]

    cvec = jnp.concatenate([c_ctx[None, :], c, jnp.zeros((MOD_ROWS - 1 - DEC_BATCH, D_MODEL), _F32)], axis=0)
    mods = _modulation(cvec, w_mod, b_mod)

    w_in_b, w_mlp1_b = w_in[0].astype(_BF), w_mlp1[0].astype(_BF)

    mats_c, mats_l = _dft_mats(SEQ), _dft_mats(DEC_SEQ)
    fargs = (b_ffn_w1, b_ffn_b1, b_ffn_w2, b_ffn_b2, b_ffn_w3, b_freq, b_delta, b_skip)
    filt_c = _hyena_filters(SEQ, mats_c, *fargs)
    filt_l = _hyena_filters(DEC_SEQ, mats_l, *fargs)
    rope = _rope_tables(DEC_SEQ)

    x = jnp.concatenate([x_prompt.reshape(N_CTX, D_MODEL), x_sample.reshape(N_LAT, D_MODEL)], axis=0)
    outs = (None,) * 5
    for l in range(DEPTH):
        lam_init = 0.8 - 0.6 * math.exp(-0.3 * l)
        lp = dict(a_lam=a_lam[l], a_subln=a_subln[l], b_conv_w=b_conv_w[l], b_conv_b=b_conv_b[l].reshape(1, 3 * B_W),
                  c_lb=lb_all[l], c_gnorm=c_gnorm[l].reshape(1, C_DV), d_qnorm=d_qnorm[l], d_knorm=d_knorm[l])
        h = _norm(x, g_norm1[l], mods[l], sc_idx=1, sh_idx=0)
        (proj,) = _matmul(h, w_in_b, name="in_proj")
        mix, outs = _mixers(proj, l, lam_init, lp, None, outs, row0=0, nb=BATCH, L=SEQ, caches=None, rope=None,
                            filt=filt_c, mats=mats_c)
        mix, outs = _mixers(proj, l, lam_init, lp, mix, outs, row0=N_CTX, nb=DEC_BATCH, L=DEC_SEQ,
                            caches=(cache_a_k, cache_a_v, cache_d_k, cache_d_v, state_c), rope=rope,
                            filt=filt_l, mats=mats_l)
        (x,) = _matmul(mix, w_out[l].astype(_BF), name="out_proj", epilogue="resid", x=x, mods=mods[l], gt_idx=2)
        h2 = _norm(x, g_norm2[l], mods[l], sc_idx=4, sh_idx=3)
        hid, w_mlp2_b = _matmul(h2, w_mlp1_b, name="mlp1", epilogue="relu2", out_dtype=_BF, side=[(w_mlp2, l)])
        nxt = [(w_mlp1, l + 1), (w_in, l + 1)] if l + 1 < DEPTH else []
        x, *cast = _matmul(hid, w_mlp2_b, name="mlp2", epilogue="resid", x=x, mods=mods[l], gt_idx=5, side=nxt)
        if cast:
            w_mlp1_b, w_in_b = cast

    y_prompt = _norm(x, g_final, row0=0, nrows=N_CTX).reshape(BATCH, SEQ, D_MODEL)
    y_sample = _norm(x, g_final, row0=N_CTX, nrows=N_LAT).reshape(DEC_BATCH, DEC_SEQ, D_MODEL)
    return (y_prompt, y_sample) + outs
```

```python
import functools
import math

import jax
import jax.numpy as jnp
import numpy as np
from jax import lax
from jax.experimental import pallas as pl
from jax.experimental.pallas import tpu as pltpu

D_MODEL = 4096
BATCH = 32
SEQ = 256
DEPTH = 4
DEC_BATCH = 4
DEC_SEQ = 1024
PAST_LEN = 512
GRID_W = 64
HEAD_DIM = 128
GROUP_W = D_MODEL // 4
A_HEADS = GROUP_W // (2 * HEAD_DIM)
B_W = GROUP_W
C_HEADS = 8
C_DK = 128
C_DV = GROUP_W // C_HEADS
C_W = C_HEADS * C_DK
D_HEADS = GROUP_W // HEAD_DIM
D_KV_HEADS = 2
D_GROUPS = D_HEADS // D_KV_HEADS
D_FF = 4 * D_MODEL
ROPE_BASE = 10000.0
ROPE_FREQS = HEAD_DIM // 4
HY_NFREQ = 16
HY_EMB = 1 + 2 * HY_NFREQ
HY_FFN = 64
HY_SHIFT = 0.05
EPS = 1e-6
GATE_FLOOR = 1e-20

N_CTX = BATCH * SEQ
N_LAT = DEC_BATCH * DEC_SEQ
N_TOK = N_CTX + N_LAT

OFF_QA = 0
OFF_KA = OFF_QA + GROUP_W
OFF_VA = OFF_KA + GROUP_W
OFF_UB = OFF_VA + GROUP_W
OFF_QC = OFF_UB + 3 * B_W
OFF_IC = OFF_QC + C_W
OFF_FF = OFF_IC + C_HEADS * C_DV
OFF_FB = OFF_FF + C_W
OFF_GZ = OFF_FB + C_W
OFF_QD = OFF_GZ + C_HEADS * C_DV
OFF_KD = OFF_QD + D_HEADS * HEAD_DIM
OFF_VD = OFF_KD + D_KV_HEADS * HEAD_DIM
IN_W = OFF_VD + D_KV_HEADS * HEAD_DIM

LANES = 128
SUBLANES = 8
VMEM_LIMIT = 56 * 1024 * 1024

CHUNK = 128

_HI = lax.Precision.HIGHEST
_BF = jnp.bfloat16
_F32 = jnp.float32


def _cparams(sem):
    return pltpu.CompilerParams(dimension_semantics=sem, vmem_limit_bytes=VMEM_LIMIT)


def _call_into(kernel_fn, bufs, *, grid, in_specs, args, out_specs, out_shape, scratch_shapes, sem, name):
    held = [b for b in bufs if b is not None]
    n = len(held)
    aliases = {}
    for oi, b in enumerate(bufs):
        if b is not None:
            aliases[len(aliases)] = oi

    def body(*refs):
        kernel_fn(*refs[n:])

    return pl.pallas_call(
        body,
        grid=grid,
        in_specs=[pl.BlockSpec(memory_space=pl.ANY)] * n + list(in_specs),
        out_specs=list(out_specs),
        out_shape=list(out_shape),
        scratch_shapes=list(scratch_shapes),
        input_output_aliases=aliases,
        compiler_params=_cparams(sem),
        name=name,
    )(*held, *args)


def _dot_nt(a, b):
    return lax.dot_general(a, b, (((1,), (1,)), ((), ())), preferred_element_type=_F32)


def _dot_tn(a, b):
    return lax.dot_general(a, b, (((0,), (0,)), ((), ())), preferred_element_type=_F32)


def _dot(a, b, precision=None):
    return jnp.dot(a, b, preferred_element_type=_F32, precision=precision)


HGRN_LEVELS = tuple(2 ** i for i in range(1, CHUNK.bit_length()))
LOG2E = 1.4426950408889634


def _hgrn_consts():
    C = CHUNK
    t = np.arange(C)[:, None]
    s = np.arange(C)[None, :]
    sgn = np.zeros((2, len(HGRN_LEVELS), C, LANES), np.float32)
    mask = np.zeros((2, len(HGRN_LEVELS) + 1, C, C), np.float32)
    tri = np.zeros((2, C, C), np.float32)
    for d in range(2):
        for li, blk in enumerate(HGRN_LEVELS):
            half = blk // 2
            q_t = ((t % blk) >= half) if d == 0 else ((t % blk) < half)
            q_s = ((s % blk) >= half) if d == 0 else ((s % blk) < half)
            sgn[d, li] = np.where(q_t, 1.0, -1.0)
            mask[d, li] = ((t // blk) == (s // blk)) & q_t & ~q_s
        mask[d, -1] = (t == s)
        tri[d] = (s <= t) if d == 0 else (s >= t)
    return jnp.asarray(tri, _BF), jnp.asarray(sgn), jnp.asarray(mask)


def _ref_rows(G, blk, reverse):
    C = G.shape[0]
    half = blk // 2
    rr = half if reverse else half - 1
    if blk >= SUBLANES:
        GB = G.reshape(C // blk, blk, LANES)
        return jnp.broadcast_to(GB[:, rr:rr + 1, :], GB.shape).reshape(C, LANES)
    if blk == 4:
        G3 = G.reshape(C // SUBLANES, SUBLANES, LANES)
        sub = lax.broadcasted_iota(jnp.int32, G3.shape, 1)
        lo = jnp.broadcast_to(G3[:, rr:rr + 1, :], G3.shape)
        hi = jnp.broadcast_to(G3[:, rr + 4:rr + 5, :], G3.shape)
        return jnp.where(sub < 4, lo, hi).reshape(C, LANES)
    odd = (lax.broadcasted_iota(jnp.int32, G.shape, 0) & 1) == 1
    if reverse:
        return jnp.where(odd, G, pltpu.roll(G, C - 1, 0))
    return jnp.where(odd, pltpu.roll(G, 1, 0), G)


def _hgrn_intra(qq, kk, vb, lf, d, tri_ref, sgn_ref, mask_ref):
    C = CHUNK
    l1 = lf.astype(_BF)
    r1 = lf - l1.astype(_F32)
    l2 = r1.astype(_BF)
    l3 = (r1 - l2.astype(_F32)).astype(_BF)
    g3 = _dot(tri_ref[d], jnp.concatenate([l1, l2, l3], axis=1))
    G2 = (g3[:, 0:LANES] + (g3[:, LANES:2 * LANES] + g3[:, 2 * LANES:3 * LANES])) * LOG2E
    scores = mask_ref[d, len(HGRN_LEVELS)] * _dot_nt(qq.astype(_BF), kk.astype(_BF))
    for li, blk in enumerate(HGRN_LEVELS):
        sg = sgn_ref[d, li]
        e = jnp.exp2((G2 - _ref_rows(G2, blk, d == 1)) * sg)
        x = (jnp.where(sg > 0.0, qq, kk) * e).astype(_BF)
        scores = scores + mask_ref[d, li] * _dot_nt(x, x)
    o = _dot(scores.astype(_BF), vb)
    qdec = (qq * jnp.exp2(G2)).astype(_BF)
    gend = G2[0:1, :] if d == 1 else G2[C - 1:C, :]
    kdec = (kk * jnp.exp2(gend - G2)).astype(_BF)
    return o, qdec, _dot_tn(vb, kdec), jnp.exp2(gend)


def _chunk_loop(n, body):
    if n <= 8:
        for c in range(n):
            body(c, 0)
    else:
        lax.fori_loop(0, n, body, 0, unroll=2)


def _hgrn2_kernel(*refs, L, has_state, emit_state, heads):
    for hh in range(heads):
        _hgrn2_head(*refs, L=L, has_state=has_state, emit_state=emit_state, hh=hh)


def _hgrn2_head(*refs, L, has_state, emit_state, hh):
    it = iter(refs)
    q_ref, v_ref, zf_ref, zb_ref, gz_ref, lb_ref, gn_ref, tri_ref, sgn_ref, mask_ref = (next(it) for _ in range(10))
    s0_ref = next(it) if has_state else None
    o_ref = next(it)
    sout_ref = next(it) if emit_state else None
    oacc, qdec_buf, ubuf, dbuf, st_ref = (next(it) for _ in range(5))
    nchunk = L // CHUNK
    ls = slice(hh * LANES, (hh + 1) * LANES)

    def rows_of(c):
        start = c * CHUNK
        return pl.ds(start if isinstance(c, int) else pl.multiple_of(start, CHUNK), CHUNK)

    def local(c, carry):
        rows = rows_of(c)
        qq = q_ref[rows, ls]
        qq = qq * jax.nn.sigmoid(qq)
        vb = v_ref[rows, ls].astype(_BF)
        for d, z_ref in enumerate((zf_ref, zb_ref)):
            z = z_ref[rows, ls]
            lb = lb_ref[d:d + 1, ls]
            lf = jnp.log(jnp.maximum(lb + (1.0 - lb) * jax.nn.sigmoid(z), GATE_FLOOR))
            kk = (1.0 - lb) * jax.nn.sigmoid(-z)
            o, qd, u, dec = _hgrn_intra(qq, kk, vb, lf, d, tri_ref, sgn_ref, mask_ref)
            oacc[d, rows, :] = o
            qdec_buf[d, rows, :] = qd
            ubuf[d, c] = u
            dbuf[d, c] = dec
        return carry

    _chunk_loop(nchunk, local)

    for d in range(2):
        st_ref[d] = s0_ref[d, hh].T if has_state else jnp.zeros((C_DV, C_DK), _F32)

    def scan(i, carry):
        for d in range(2):
            c = (nchunk - 1 - i) if d == 1 else i
            rows = rows_of(c)
            st = st_ref[d]
            oacc[d, rows, :] = oacc[d, rows, :] + _dot_nt(qdec_buf[d, rows, :], st.astype(_BF))
            st_ref[d] = st * dbuf[d, c] + ubuf[d, c]
        return carry

    _chunk_loop(nchunk, scan)
    if emit_state:
        for d in range(2):
            sout_ref[d, hh] = st_ref[d].T

    def finish(c, carry):
        rows = rows_of(c)
        tot = oacc[0, rows, :] + oacc[1, rows, :]
        g = gz_ref[rows, ls]
        o_ref[rows, ls] = (_rms_lanes(tot, gn_ref[...]) * (g * jax.nn.sigmoid(g))).astype(o_ref.dtype)
        return carry

    _chunk_loop(nchunk, finish)


MIX_COL = dict(a=0, b=GROUP_W, c=2 * GROUP_W, d=3 * GROUP_W)
MIX_SDS = jax.ShapeDtypeStruct((N_TOK, D_MODEL), _BF)


def _hgrn2(proj, lb, gnorm, s0, mix, states, *, row0, nb, L, layer):
    has_state = s0 is not None
    emit_state = not has_state
    rb0 = row0 // L
    nchunk = L // CHUNK
    tri, sgn, mask = _hgrn_consts()

    heads = max(1, DEC_SEQ // L)
    cw = heads * LANES

    def col(off):
        return pl.BlockSpec((L, cw), lambda b, h, off=off: (rb0 + b, off // cw + h))

    in_specs = [col(OFF_QC), col(OFF_IC), col(OFF_FF), col(OFF_FB), col(OFF_GZ),
                pl.BlockSpec((2, cw), lambda b, h: (0, h)),
                pl.BlockSpec((1, C_DV), lambda b, h: (0, 0)),
                _const_spec(tri.shape), _const_spec(sgn.shape), _const_spec(mask.shape)]
    args = [proj, proj, proj, proj, proj, lb, gnorm, tri, sgn, mask]
    st_spec = pl.BlockSpec((None, None, 2, heads, C_DK, C_DV), lambda b, h: (b, layer, 0, h, 0, 0))
    if has_state:
        in_specs.append(st_spec)
        args.append(s0)
    out_shape = [MIX_SDS]
    out_specs = [pl.BlockSpec((L, cw), lambda b, h: (rb0 + b, MIX_COL["c"] // cw + h))]
    bufs = [mix]
    if emit_state:
        out_shape.append(jax.ShapeDtypeStruct((nb, DEPTH, 2, C_HEADS, C_DK, C_DV), _F32))
        out_specs.append(st_spec)
        bufs.append(states)
    res = _call_into(
        functools.partial(_hgrn2_kernel, L=L, has_state=has_state, emit_state=emit_state, heads=heads), bufs,
        grid=(nb, C_HEADS // heads),
        in_specs=in_specs,
        args=args,
        out_specs=out_specs,
        out_shape=out_shape,
        scratch_shapes=[pltpu.VMEM((2, L, C_DV), _F32), pltpu.VMEM((2, L, C_DK), _BF),
                        pltpu.VMEM((2, nchunk, C_DV, C_DK), _F32), pltpu.VMEM((2, nchunk, 1, C_DK), _F32),
                        pltpu.VMEM((2, C_DV, C_DK), _F32)],
        sem=("parallel", "parallel"),
        name="hgrn2_lat" if has_state else "hgrn2_ctx",
    )
    return (res[0], res[1]) if emit_state else (res[0], states)


MOD_ROWS = 8
MOD_TN = 512


def _mod_row(i, tm):
    start = i * tm
    return jnp.where(start < N_CTX, 0, 1 + (start - N_CTX) // DEC_SEQ)


def _mod_kernel(c_ref, w_ref, b_ref, o_ref):
    cv = c_ref[...]
    s = (cv * jax.nn.sigmoid(cv)).astype(_BF)
    o_ref[...] = _dot(s, w_ref[...].astype(_BF)) + b_ref[...]


def _modulation(cvec, w_mod, b_mod):
    n = 6 * D_MODEL
    out = pl.pallas_call(
        _mod_kernel,
        grid=(DEPTH, n // MOD_TN),
        in_specs=[pl.BlockSpec((MOD_ROWS, D_MODEL), lambda l, j: (0, 0)),
                  pl.BlockSpec((None, D_MODEL, MOD_TN), lambda l, j: (l, 0, j)),
                  pl.BlockSpec((None, 1, MOD_TN), lambda l, j: (l, 0, j))],
        out_specs=pl.BlockSpec((None, MOD_ROWS, MOD_TN), lambda l, j: (l, 0, j)),
        out_shape=jax.ShapeDtypeStruct((DEPTH, MOD_ROWS, n), _F32),
        compiler_params=_cparams(("parallel", "parallel")),
        name="modulation",
    )(cvec, w_mod, b_mod.reshape(DEPTH, 1, n))
    return out.reshape(DEPTH, MOD_ROWS, 6, 1, D_MODEL).transpose(0, 2, 1, 3, 4)


NORM_TM = 512


def _norm_kernel(x_ref, g_ref, *rest, modulate):
    x = x_ref[...]
    y = x * lax.rsqrt(jnp.mean(x * x, axis=-1, keepdims=True) + EPS) * g_ref[...]
    if modulate:
        sc_ref, sh_ref, o_ref = rest
        y = y * (1.0 + sc_ref[...]) + sh_ref[...]
    else:
        (o_ref,) = rest
    o_ref[...] = y.astype(o_ref.dtype)


def _norm(x, g, mods=None, sc_idx=0, sh_idx=0, row0=0, nrows=N_TOK):
    modulate = mods is not None
    tm = NORM_TM
    t0 = row0 // tm
    in_specs = [pl.BlockSpec((tm, D_MODEL), lambda i: (t0 + i, 0)),
                pl.BlockSpec((1, D_MODEL), lambda i: (0, 0))]
    args = [x, g.reshape(1, D_MODEL)]
    if modulate:
        in_specs += [pl.BlockSpec((None, None, 1, D_MODEL), lambda i: (sc_idx, _mod_row(t0 + i, tm), 0, 0)),
                     pl.BlockSpec((None, None, 1, D_MODEL), lambda i: (sh_idx, _mod_row(t0 + i, tm), 0, 0))]
        args += [mods, mods]
    return pl.pallas_call(
        functools.partial(_norm_kernel, modulate=modulate),
        grid=(nrows // tm,),
        in_specs=in_specs,
        out_specs=pl.BlockSpec((tm, D_MODEL), lambda i: (i, 0)),
        out_shape=jax.ShapeDtypeStruct((nrows, D_MODEL), _BF if modulate else _F32),
        compiler_params=_cparams(("parallel",)),
        name="norm_mod" if modulate else "norm_final",
    )(*args)


def _mm_kernel(a_ref, b_ref, *rest, epilogue, nk, grid, side_chunks):
    ns = len(side_chunks)
    if epilogue == "resid":
        x_ref, gt_ref = rest[:2]
        rest = rest[2:]
    src_refs, o_ref, dst_refs, rest = rest[:ns], rest[ns], rest[ns + 1:2 * ns + 1], rest[2 * ns + 1:]

    step = (pl.program_id(0) * grid[1] + pl.program_id(1)) * grid[2] + pl.program_id(2)
    for src_ref, dst_ref, nchunks in zip(src_refs, dst_refs, side_chunks):
        @pl.when(step < nchunks)
        def _(src_ref=src_ref, dst_ref=dst_ref):
            dst_ref[...] = src_ref[...].astype(dst_ref.dtype)

    def finish(acc):
        if epilogue == "resid":
            o_ref[...] = x_ref[...] + gt_ref[...] * acc
        elif epilogue == "relu2":
            o_ref[...] = jnp.square(jnp.maximum(acc, 0.0)).astype(o_ref.dtype)
        else:
            o_ref[...] = acc.astype(o_ref.dtype)

    if nk == 1:
        finish(_dot(a_ref[...], b_ref[...]))
        return
    (acc_ref,) = rest
    k = pl.program_id(2)

    @pl.when(k == 0)
    def _():
        acc_ref[...] = jnp.zeros_like(acc_ref)

    acc_ref[...] += _dot(a_ref[...], b_ref[...])

    @pl.when(k == nk - 1)
    def _():
        finish(acc_ref[...])


MM_TILES = dict(in_proj=(1024, 1280, D_MODEL), out_proj=(1024, 1024, D_MODEL),
                mlp1=(1024, 1024, D_MODEL), mlp2=(1024, 1024, 2048))


SIDE_CHUNKS = 128


def _matmul(a, b, *, name, epilogue="plain", out_dtype=_F32, x=None, mods=None, gt_idx=0, side=()):
    M, K = a.shape
    N = b.shape[1]
    tm, tn, tk = MM_TILES[name]
    nk = K // tk
    grid = (M // tm, N // tn, nk)
    assert not side or grid[0] * grid[1] * grid[2] >= SIDE_CHUNKS
    in_specs = [pl.BlockSpec((tm, tk), lambda i, j, k: (i, k)),
                pl.BlockSpec((tk, tn), lambda i, j, k: (k, j))]
    args = [a, b]
    if epilogue == "resid":
        in_specs += [pl.BlockSpec((tm, tn), lambda i, j, k: (i, j)),
                     pl.BlockSpec((None, None, 1, tn), lambda i, j, k: (gt_idx, _mod_row(i, tm), 0, j))]
        args += [x, mods]

    def chunk(i, j, k):
        return jnp.minimum((i * grid[1] + j) * grid[2] + k, SIDE_CHUNKS - 1)

    out_specs = [pl.BlockSpec((tm, tn), lambda i, j, k: (i, j))]
    out_shape = [jax.ShapeDtypeStruct((M, N), out_dtype)]
    for w, layer in side:
        _, R, C = w.shape
        rows = R // SIDE_CHUNKS
        in_specs.append(pl.BlockSpec((None, rows, C), lambda i, j, k, layer=layer: (layer, chunk(i, j, k), 0)))
        args.append(w)
        out_specs.append(pl.BlockSpec((rows, C), lambda i, j, k: (chunk(i, j, k), 0)))
        out_shape.append(jax.ShapeDtypeStruct((R, C), _BF))
    return pl.pallas_call(
        functools.partial(_mm_kernel, epilogue=epilogue, nk=nk, grid=grid, side_chunks=(SIDE_CHUNKS,) * len(side)),
        grid=grid,
        in_specs=in_specs,
        out_specs=out_specs,
        out_shape=out_shape,
        scratch_shapes=[pltpu.VMEM((tm, tn), _F32)] if nk > 1 else [],
        compiler_params=_cparams(("arbitrary", "arbitrary", "arbitrary")),
        name=name,
    )(*args)


ATT_TQ = 256
ATT_SCALE = HEAD_DIM ** -0.5


def _rope(x, cos, sin):
    lane = lax.broadcasted_iota(jnp.int32, x.shape, 1)
    first = (lane % (2 * ROPE_FREQS)) < ROPE_FREQS
    partner = jnp.where(first, pltpu.roll(x, HEAD_DIM - ROPE_FREQS, 1), pltpu.roll(x, ROPE_FREQS, 1))
    return x * cos + partner * sin


def _rope_tables(L):
    pos = jnp.arange(L)
    rows = (pos // GRID_W).astype(_F32)
    cols = (pos % GRID_W).astype(_F32)
    inv = ROPE_BASE ** (-jnp.arange(ROPE_FREQS, dtype=_F32) / ROPE_FREQS)
    ang_r = rows[:, None] * inv
    ang_c = cols[:, None] * inv
    cos = jnp.concatenate([jnp.cos(ang_r)] * 2 + [jnp.cos(ang_c)] * 2, axis=-1)
    sin = jnp.concatenate([-jnp.sin(ang_r), jnp.sin(ang_r), -jnp.sin(ang_c), jnp.sin(ang_c)], axis=-1)
    return cos, sin


Q_PRESCALE = ATT_SCALE * LOG2E


def _softmax_parts(s2):
    e = jnp.exp2(s2 - jnp.max(s2, axis=-1, keepdims=True))
    return e, 1.0 / jnp.sum(e, axis=-1, keepdims=True)


def _rms_lanes(x, g):
    return x * lax.rsqrt(jnp.mean(x * x, axis=-1, keepdims=True) + EPS) * g


def _diff_kernel(*refs, L, latent, lam_init):
    it = iter(refs)
    q_ref, k_ref, v_ref = next(it), next(it), next(it)
    if latent:
        ck_ref, cv_ref, cos_ref, sin_ref = next(it), next(it), next(it), next(it)
    lam_ref, sub_ref, o_ref = next(it), next(it), next(it)
    if not latent:
        ak_ref, av_ref = next(it), next(it)
    kbuf, vbuf = next(it), next(it)
    w = 2 * HEAD_DIM

    lp = lam_ref[...]
    lam = (jnp.exp(jnp.sum(lp[0:1] * lp[1:2], axis=-1, keepdims=True))
           - jnp.exp(jnp.sum(lp[2:3] * lp[3:4], axis=-1, keepdims=True)) + lam_init)
    for h in range(1 if latent else A_HEADS):
        base = h * w
        for cpt in range(2):
            cs = slice(base + cpt * HEAD_DIM, base + (cpt + 1) * HEAD_DIM)
            kc = k_ref[:, cs]
            if latent:
                kc = _rope(kc, cos_ref[...], sin_ref[...])
                kbuf[cpt, L:, :] = ck_ref[:, cs].astype(_BF)
            kbuf[cpt, 0:L, :] = kc.astype(_BF)
        vv = v_ref[:, base:base + w]
        vbuf[0:L, :] = vv.astype(_BF)
        if latent:
            vbuf[L:, :] = cv_ref[...].astype(_BF)
        else:
            ak_ref[:, h, :] = k_ref[:, base:base + w]
            av_ref[:, h, :] = vv

        for qi in range(L // ATT_TQ):
            rows = slice(qi * ATT_TQ, (qi + 1) * ATT_TQ)
            parts = []
            for cpt in range(2):
                qc = q_ref[rows, base + cpt * HEAD_DIM:base + (cpt + 1) * HEAD_DIM]
                if latent:
                    qc = _rope(qc, cos_ref[rows, :], sin_ref[rows, :])
                parts.append(_softmax_parts(_dot_nt((qc * Q_PRESCALE).astype(_BF), kbuf[cpt])))
            (e0, r0), (e1, r1) = parts
            a = (e0 * r0 - e1 * (lam * r1)).astype(_BF)
            o = _dot(a, vbuf[...])
            o_ref[rows, base:base + w] = (_rms_lanes(o, sub_ref[...]) * (1.0 - lam_init)).astype(o_ref.dtype)


def _diff_attn(proj, lam_p, subln, lam_init, mix, new_k, new_v, *, row0, nb, L, layer,
               cache_k=None, cache_v=None, rope=None):
    latent = cache_k is not None
    rb0 = row0 // L
    w = 2 * HEAD_DIM
    nkeys = L + (PAST_LEN if latent else 0)
    cw = w if latent else GROUP_W

    def col(off):
        return pl.BlockSpec((L, cw), lambda b, h, off=off: (rb0 + b, off // cw + h))

    in_specs = [col(OFF_QA), col(OFF_KA), col(OFF_VA)]
    args = [proj, proj, proj]
    if latent:
        cspec = pl.BlockSpec((None, None, PAST_LEN, w), lambda b, h: (b, layer, 0, h))
        tspec = pl.BlockSpec((L, HEAD_DIM), lambda b, h: (0, 0))
        in_specs += [cspec, cspec, tspec, tspec]
        args += [cache_k.reshape(nb, DEPTH, PAST_LEN, GROUP_W), cache_v.reshape(nb, DEPTH, PAST_LEN, GROUP_W), rope[0], rope[1]]
    in_specs += [pl.BlockSpec((4, HEAD_DIM), lambda b, h: (0, 0)), pl.BlockSpec((1, w), lambda b, h: (0, 0))]
    args += [lam_p, subln.reshape(1, w)]
    out_shape = [MIX_SDS]
    out_specs = [pl.BlockSpec((L, cw), lambda b, h: (rb0 + b, MIX_COL["a"] // cw + h))]
    bufs = [mix]
    if not latent:
        kv_sds = jax.ShapeDtypeStruct((nb, DEPTH, L, A_HEADS, w), _F32)
        kv_spec = pl.BlockSpec((None, None, L, A_HEADS, w), lambda b, h: (b, layer, 0, 0, 0))
        out_shape += [kv_sds, kv_sds]
        out_specs += [kv_spec, kv_spec]
        bufs += [new_k, new_v]
    res = _call_into(
        functools.partial(_diff_kernel, L=L, latent=latent, lam_init=lam_init), bufs,
        grid=(nb, A_HEADS if latent else 1),
        in_specs=in_specs,
        args=args,
        out_specs=out_specs,
        out_shape=out_shape,
        scratch_shapes=[pltpu.VMEM((2, nkeys, HEAD_DIM), _BF), pltpu.VMEM((nkeys, w), _BF)],
        sem=("parallel", "parallel"),
        name="diff_attn_lat" if latent else "diff_attn_ctx",
    )
    return (res[0], new_k, new_v) if latent else tuple(res)


def _gqa_kernel(*refs, L, latent):
    it = iter(refs)
    q_ref, k_ref, v_ref = next(it), next(it), next(it)
    if latent:
        ck_ref, cv_ref, cos_ref, sin_ref = next(it), next(it), next(it), next(it)
    qn_ref, kn_ref, o_ref = next(it), next(it), next(it)
    if not latent:
        dk_ref, dv_ref = next(it), next(it)
    kbuf, vbuf = next(it), next(it)
    wq = D_GROUPS * HEAD_DIM

    for n in range(1 if latent else D_KV_HEADS):
        ks = slice(n * HEAD_DIM, (n + 1) * HEAD_DIM)
        kn = _rms_lanes(k_ref[:, ks], kn_ref[...])
        vv = v_ref[:, ks]
        if latent:
            kn = _rope(kn, cos_ref[...], sin_ref[...])
            kbuf[L:, :] = ck_ref[...].astype(_BF)
            vbuf[L:, :] = cv_ref[...].astype(_BF)
        else:
            dk_ref[:, n, :] = kn
            dv_ref[:, n, :] = vv
        kbuf[0:L, :] = kn.astype(_BF)
        vbuf[0:L, :] = vv.astype(_BF)

        for g in range(D_GROUPS):
            cs = slice(n * wq + g * HEAD_DIM, n * wq + (g + 1) * HEAD_DIM)
            for qi in range(L // ATT_TQ):
                rows = slice(qi * ATT_TQ, (qi + 1) * ATT_TQ)
                qg = _rms_lanes(q_ref[rows, cs], qn_ref[...])
                if latent:
                    qg = _rope(qg, cos_ref[rows, :], sin_ref[rows, :])
                e, r = _softmax_parts(_dot_nt((qg * Q_PRESCALE).astype(_BF), kbuf[...]))
                o_ref[rows, cs] = (_dot(e.astype(_BF), vbuf[...]) * r).astype(o_ref.dtype)


def _gqa(proj, qnorm, knorm, mix, new_k, new_v, *, row0, nb, L, layer, cache_k=None, cache_v=None, rope=None):
    latent = cache_k is not None
    rb0 = row0 // L
    nh = 1 if latent else D_KV_HEADS
    wq = nh * D_GROUPS * HEAD_DIM
    wk = nh * HEAD_DIM
    nkeys = L + (PAST_LEN if latent else 0)
    in_specs = [pl.BlockSpec((L, wq), lambda b, n: (rb0 + b, OFF_QD // wq + n)),
                pl.BlockSpec((L, wk), lambda b, n: (rb0 + b, OFF_KD // wk + n)),
                pl.BlockSpec((L, wk), lambda b, n: (rb0 + b, OFF_VD // wk + n))]
    args = [proj, proj, proj]
    if latent:
        cspec = pl.BlockSpec((None, None, PAST_LEN, HEAD_DIM), lambda b, n: (b, layer, 0, n))
        tspec = pl.BlockSpec((L, HEAD_DIM), lambda b, n: (0, 0))
        in_specs += [cspec, cspec, tspec, tspec]
        kvw = D_KV_HEADS * HEAD_DIM
        args += [cache_k.reshape(nb, DEPTH, PAST_LEN, kvw), cache_v.reshape(nb, DEPTH, PAST_LEN, kvw), rope[0], rope[1]]
    gspec = pl.BlockSpec((1, HEAD_DIM), lambda b, n: (0, 0))
    in_specs += [gspec, gspec]
    args += [qnorm.reshape(1, HEAD_DIM), knorm.reshape(1, HEAD_DIM)]
    out_shape = [MIX_SDS]
    out_specs = [pl.BlockSpec((L, wq), lambda b, n: (rb0 + b, MIX_COL["d"] // wq + n))]
    bufs = [mix]
    if not latent:
        kv_sds = jax.ShapeDtypeStruct((nb, DEPTH, L, D_KV_HEADS, HEAD_DIM), _F32)
        kv_spec = pl.BlockSpec((None, None, L, D_KV_HEADS, HEAD_DIM), lambda b, n: (b, layer, 0, 0, 0))
        out_shape += [kv_sds, kv_sds]
        out_specs += [kv_spec, kv_spec]
        bufs += [new_k, new_v]
    res = _call_into(
        functools.partial(_gqa_kernel, L=L, latent=latent), bufs,
        grid=(nb, D_KV_HEADS // nh),
        in_specs=in_specs,
        args=args,
        out_specs=out_specs,
        out_shape=out_shape,
        scratch_shapes=[pltpu.VMEM((nkeys, HEAD_DIM), _BF), pltpu.VMEM((nkeys, HEAD_DIM), _BF)],
        sem=("parallel", "parallel"),
        name="gqa_lat" if latent else "gqa_ctx",
    )
    return (res[0], new_k, new_v) if latent else tuple(res)


HY_CB = 256


def _dft_mats(L):
    f = np.arange(L, dtype=np.float64)[:, None] + 0.5
    t = np.arange(L, dtype=np.float64)[None, :]
    ang = 2.0 * np.pi * f * t / (2 * L)
    out = []
    for m in (np.cos(ang), np.sin(ang), np.cos(ang).T, np.sin(ang).T):
        m32 = jnp.asarray(m.astype(np.float32))
        hi = m32.astype(_BF)
        out += [hi, (m32 - hi.astype(_F32)).astype(_BF)]
    return tuple(out)


def _split_bf16(x):
    hi = x.astype(_BF)
    return hi, (x - hi.astype(_F32)).astype(_BF)


def _dot3(m_hi, m_lo, x):
    xh, xl = _split_bf16(x)
    return _dot(m_hi, xh) + (_dot(m_hi, xl) + _dot(m_lo, xh))


def _filter_features(L):
    t = jnp.arange(L, dtype=_F32) / L
    bands = jnp.arange(1, HY_NFREQ + 1, dtype=_F32)
    ang = 2.0 * math.pi * t[:, None] * bands
    z = jnp.concatenate([t[:, None], jnp.cos(ang), jnp.sin(ang)], axis=-1)
    return z, t[:, None]


def _filt_kernel(z_ref, t_ref, w1_ref, b1_ref, w2_ref, b2_ref, w3f_ref, w3b_ref, fr_ref, df_ref, db_ref, sk_ref,
                 cmh_ref, cml_ref, smh_ref, sml_ref, hr_ref, hi_ref, *, L):
    fr = fr_ref[...]
    hid = jnp.sin(fr * (_dot(z_ref[...], w1_ref[...], _HI) + b1_ref[...]))
    hid = jnp.sin(fr * (_dot(hid, w2_ref[...], _HI) + b2_ref[...]))
    t = t_ref[...]

    def filt(w3_ref, d_ref):
        h = _dot(hid, w3_ref[...], _HI) * (jnp.exp(-t * jnp.abs(d_ref[...])) + HY_SHIFT)
        return h / (jnp.sum(jnp.abs(h), axis=0, keepdims=True) + EPS)

    hf = filt(w3f_ref, df_ref)
    hb = filt(w3b_ref, db_ref)
    hr_ref[...] = (_dot3(cmh_ref[...], cml_ref[...], hf + hb) + sk_ref[...]) * (1.0 / L)
    hi_ref[...] = _dot3(smh_ref[...], sml_ref[...], hb - hf) * (1.0 / L)


def _const_spec(shape):
    zeros = (0,) * len(shape)
    return pl.BlockSpec(shape, lambda *_: zeros, pipeline_mode=pl.Buffered(1))


def _hyena_filters(L, mats, w1, b1, w2, b2, w3, freq, delta, skip):
    z, t = _filter_features(L)
    cb = 2 * HY_CB
    ncb = B_W // cb
    full = lambda shape: pl.BlockSpec(shape, lambda l, o, j: (0,) * len(shape))
    per_layer = lambda shape: pl.BlockSpec((None,) + shape, lambda l, o, j: (l,) + (0,) * len(shape))
    mspec = _const_spec((L, L))
    in_specs = [full((L, HY_EMB)), full((L, 1)),
                per_layer((HY_EMB, HY_FFN)), per_layer((1, HY_FFN)),
                per_layer((HY_FFN, HY_FFN)), per_layer((1, HY_FFN)),
                pl.BlockSpec((None, HY_FFN, cb), lambda l, o, j: (l, 0, o * ncb + j)),
                pl.BlockSpec((None, HY_FFN, cb), lambda l, o, j: (l, 0, (2 + o) * ncb + j)),
                per_layer((1, HY_FFN)),
                pl.BlockSpec((None, None, 1, cb), lambda l, o, j: (l, o, 0, j)),
                pl.BlockSpec((None, None, 1, cb), lambda l, o, j: (l, 2 + o, 0, j)),
                pl.BlockSpec((None, None, 1, cb), lambda l, o, j: (l, o, 0, j)),
                mspec, mspec, mspec, mspec]
    out_spec = pl.BlockSpec((None, None, L, cb), lambda l, o, j: (l, o, 0, j))
    out_sds = jax.ShapeDtypeStruct((DEPTH, 2, L, B_W), _F32)
    return pl.pallas_call(
        functools.partial(_filt_kernel, L=L),
        grid=(DEPTH, 2, ncb),
        in_specs=in_specs,
        out_specs=[out_spec, out_spec],
        out_shape=[out_sds, out_sds],
        compiler_params=_cparams(("parallel", "parallel", "parallel")),
        name=f"hyena_filters_{L}",
    )(z, t, w1, b1.reshape(DEPTH, 1, HY_FFN), w2, b2.reshape(DEPTH, 1, HY_FFN), w3, w3,
      freq.reshape(DEPTH, 1, HY_FFN), delta.reshape(DEPTH, 4, 1, B_W), delta.reshape(DEPTH, 4, 1, B_W),
      skip.reshape(DEPTH, 2, 1, B_W), *mats[:4])


def _short_conv(u, w, b):
    L = u.shape[0]
    row = lax.broadcasted_iota(jnp.int32, u.shape, 0)
    prev = jnp.where(row == 0, 0.0, pltpu.roll(u, 1, 0))
    nxt = jnp.where(row == L - 1, 0.0, pltpu.roll(u, L - 1, 0))
    return prev * w[0:1, :] + u * w[1:2, :] + nxt * w[2:3, :] + b


def _hyena_kernel(v_ref, x1_ref, x2_ref, wv_ref, w1_ref, w2_ref, bv_ref, b1_ref, b2_ref, hr_ref, hi_ref,
                  cm_ref, sm_ref, cmt_ref, smt_ref, o_ref):
    def conv(u, o):
        ub = u.astype(_BF)
        a = _dot(cm_ref[...], ub)
        b = _dot(sm_ref[...], ub)
        hr = hr_ref[o]
        hi = hi_ref[o]
        return (_dot(cmt_ref[...], (a * hr + b * hi).astype(_BF))
                - _dot(smt_ref[...], (a * hi - b * hr).astype(_BF)))

    v = _short_conv(v_ref[...], wv_ref[...], bv_ref[...])
    z = _short_conv(x1_ref[...], w1_ref[...], b1_ref[...]) * conv(v, 0)
    y = _short_conv(x2_ref[...], w2_ref[...], b2_ref[...]) * conv(z, 1)
    o_ref[...] = y.astype(o_ref.dtype)


def _hyena(proj, conv_w, conv_b, hr, hi, mats, mix, *, row0, nb, L, layer):
    rb0 = row0 // L
    cb = min(B_W, HY_CB * DEC_SEQ // L)
    ncb = B_W // cb

    def ucol(part):
        return pl.BlockSpec((L, cb), lambda b, j, part=part: (rb0 + b, OFF_UB // cb + part * ncb + j))

    def wcol(rows, part):
        return pl.BlockSpec((rows, cb), lambda b, j, part=part: (0, part * ncb + j))

    hspec = pl.BlockSpec((None, 2, L, cb), lambda b, j: (layer, 0, 0, j))
    mspec = _const_spec((L, L))
    return _call_into(
        _hyena_kernel, [mix],
        grid=(nb, ncb),
        in_specs=[ucol(0), ucol(1), ucol(2), wcol(3, 0), wcol(3, 1), wcol(3, 2), wcol(1, 0), wcol(1, 1), wcol(1, 2),
                  hspec, hspec] + [mspec] * 4,
        args=[proj, proj, proj, conv_w, conv_w, conv_w, conv_b, conv_b, conv_b, hr, hi, *mats[0::2]],
        out_specs=[pl.BlockSpec((L, cb), lambda b, j: (rb0 + b, MIX_COL["b"] // cb + j))],
        out_shape=[MIX_SDS],
        scratch_shapes=[],
        sem=("parallel", "parallel"),
        name=f"hyena_{L}",
    )[0]


def _mixers(proj, l, lam_init, lp, mix, outs, *, row0, nb, L, caches, rope, filt, mats):
    kw = dict(row0=row0, nb=nb, L=L, layer=l)
    ak, av, dk, dv, st = outs
    if caches is not None:
        att = dict(rope=rope, **kw)
        mix, _, _ = _diff_attn(proj, lp["a_lam"], lp["a_subln"], lam_init, mix, ak, av,
                               cache_k=caches[0], cache_v=caches[1], **att)
        mix, _, _ = _gqa(proj, lp["d_qnorm"], lp["d_knorm"], mix, dk, dv, cache_k=caches[2], cache_v=caches[3], **att)
        mix, _ = _hgrn2(proj, lp["c_lb"], lp["c_gnorm"], caches[4], mix, st, **kw)
    else:
        mix, ak, av = _diff_attn(proj, lp["a_lam"], lp["a_subln"], lam_init, mix, ak, av, **kw)
        mix, dk, dv = _gqa(proj, lp["d_qnorm"], lp["d_knorm"], mix, dk, dv, **kw)
        mix, st = _hgrn2(proj, lp["c_lb"], lp["c_gnorm"], None, mix, st, **kw)
    mix = _hyena(proj, lp["b_conv_w"], lp["b_conv_b"], filt[0], filt[1], mats, mix, **kw)
    return mix, (ak, av, dk, dv, st)


def kernel(x_prompt, x_sample, c, cache_a_k, cache_a_v, cache_d_k, cache_d_v, state_c, c_ctx, w_mod, b_mod, g_norm1, g_norm2, w_in, w_out, a_lam, a_subln, b_conv_w, b_conv_b, b_ffn_w1, b_ffn_b1, b_ffn_w2, b_ffn_b2, b_ffn_w3, b_freq, b_delta, b_skip, c_lb_raw, c_gnorm, d_qnorm, d_knorm, w_mlp1, w_mlp2, g_final):
    p = jax.nn.softmax(c_lb_raw.astype(_F32), axis=0)
    lb_all = jnp.cumsum(p, axis=0) - p[:1]

    cvec = jnp.concatenate([c_ctx[None, :], c, jnp.zeros((MOD_ROWS - 1 - DEC_BATCH, D_MODEL), _F32)], axis=0)
    mods = _modulation(cvec, w_mod, b_mod)

    w_in_b, w_out_b, w_mlp1_b = w_in[0].astype(_BF), w_out[0].astype(_BF), w_mlp1[0].astype(_BF)

    mats_c, mats_l = _dft_mats(SEQ), _dft_mats(DEC_SEQ)
    fargs = (b_ffn_w1, b_ffn_b1, b_ffn_w2, b_ffn_b2, b_ffn_w3, b_freq, b_delta, b_skip)
    filt_c = _hyena_filters(SEQ, mats_c, *fargs)
    filt_l = _hyena_filters(DEC_SEQ, mats_l, *fargs)
    rope = _rope_tables(DEC_SEQ)

    x = jnp.concatenate([x_prompt.reshape(N_CTX, D_MODEL), x_sample.reshape(N_LAT, D_MODEL)], axis=0)
    outs = (None,) * 5
    for l in range(DEPTH):
        lam_init = 0.8 - 0.6 * math.exp(-0.3 * l)
        lp = dict(a_lam=a_lam[l], a_subln=a_subln[l], b_conv_w=b_conv_w[l], b_conv_b=b_conv_b[l].reshape(1, 3 * B_W),
                  c_lb=lb_all[l], c_gnorm=c_gnorm[l].reshape(1, C_DV), d_qnorm=d_qnorm[l], d_knorm=d_knorm[l])
        h = _norm(x, g_norm1[l], mods[l], sc_idx=1, sh_idx=0)
        (proj,) = _matmul(h, w_in_b, name="in_proj")
        mix, outs = _mixers(proj, l, lam_init, lp, None, outs, row0=0, nb=BATCH, L=SEQ, caches=None, rope=None,
                            filt=filt_c, mats=mats_c)
        mix, outs = _mixers(proj, l, lam_init, lp, mix, outs, row0=N_CTX, nb=DEC_BATCH, L=DEC_SEQ,
                            caches=(cache_a_k, cache_a_v, cache_d_k, cache_d_v, state_c), rope=rope,
                            filt=filt_l, mats=mats_l)
        (x,) = _matmul(mix, w_out_b, name="out_proj", epilogue="resid", x=x, mods=mods[l], gt_idx=2)
        h2 = _norm(x, g_norm2[l], mods[l], sc_idx=4, sh_idx=3)
        more = l + 1 < DEPTH
        hid, w_mlp2_b, *cast = _matmul(h2, w_mlp1_b, name="mlp1", epilogue="relu2", out_dtype=_BF,
                                       side=[(w_mlp2, l)] + ([(w_in, l + 1)] if more else []))
        if more:
            (w_in_b,) = cast
        x, *cast = _matmul(hid, w_mlp2_b, name="mlp2", epilogue="resid", x=x, mods=mods[l], gt_idx=5,
                           side=[(w_mlp1, l + 1), (w_out, l + 1)] if more else [])
        if more:
            w_mlp1_b, w_out_b = cast

    y_prompt = _norm(x, g_final, row0=0, nrows=N_CTX).reshape(BATCH, SEQ, D_MODEL)
    y_sample = _norm(x, g_final, row0=N_CTX, nrows=N_LAT).reshape(DEC_BATCH, DEC_SEQ, D_MODEL)
    return (y_prompt, y_sample) + outs
```

```python
import functools
import math

import jax
import jax.numpy as jnp
import numpy as np
from jax import lax
from jax.experimental import pallas as pl
from jax.experimental.pallas import tpu as pltpu

D_MODEL = 4096
BATCH = 32
SEQ = 256
DEPTH = 4
DEC_BATCH = 4
DEC_SEQ = 1024
PAST_LEN = 512
GRID_W = 64
HEAD_DIM = 128
GROUP_W = D_MODEL // 4
A_HEADS = GROUP_W // (2 * HEAD_DIM)
B_W = GROUP_W
C_HEADS = 8
C_DK = 128
C_DV = GROUP_W // C_HEADS
C_W = C_HEADS * C_DK
D_HEADS = GROUP_W // HEAD_DIM
D_KV_HEADS = 2
D_GROUPS = D_HEADS // D_KV_HEADS
D_FF = 4 * D_MODEL
ROPE_BASE = 10000.0
ROPE_FREQS = HEAD_DIM // 4
HY_NFREQ = 16
HY_EMB = 1 + 2 * HY_NFREQ
HY_FFN = 64
HY_SHIFT = 0.05
EPS = 1e-6
GATE_FLOOR = 1e-20

N_CTX = BATCH * SEQ
N_LAT = DEC_BATCH * DEC_SEQ
N_TOK = N_CTX + N_LAT

OFF_QA = 0
OFF_KA = OFF_QA + GROUP_W
OFF_VA = OFF_KA + GROUP_W
OFF_UB = OFF_VA + GROUP_W
OFF_QC = OFF_UB + 3 * B_W
OFF_IC = OFF_QC + C_W
OFF_FF = OFF_IC + C_HEADS * C_DV
OFF_FB = OFF_FF + C_W
OFF_GZ = OFF_FB + C_W
OFF_QD = OFF_GZ + C_HEADS * C_DV
OFF_KD = OFF_QD + D_HEADS * HEAD_DIM
OFF_VD = OFF_KD + D_KV_HEADS * HEAD_DIM
IN_W = OFF_VD + D_KV_HEADS * HEAD_DIM

LANES = 128
SUBLANES = 8
VMEM_LIMIT = 56 * 1024 * 1024

CHUNK = 128

_HI = lax.Precision.HIGHEST
_BF = jnp.bfloat16
_F32 = jnp.float32


def _cparams(sem):
    return pltpu.CompilerParams(dimension_semantics=sem, vmem_limit_bytes=VMEM_LIMIT)


def _call_into(kernel_fn, bufs, *, grid, in_specs, args, out_specs, out_shape, scratch_shapes, sem, name):
    held = [b for b in bufs if b is not None]
    n = len(held)
    aliases = {}
    for oi, b in enumerate(bufs):
        if b is not None:
            aliases[len(aliases)] = oi

    def body(*refs):
        kernel_fn(*refs[n:])

    return pl.pallas_call(
        body,
        grid=grid,
        in_specs=[pl.BlockSpec(memory_space=pl.ANY)] * n + list(in_specs),
        out_specs=list(out_specs),
        out_shape=list(out_shape),
        scratch_shapes=list(scratch_shapes),
        input_output_aliases=aliases,
        compiler_params=_cparams(sem),
        name=name,
    )(*held, *args)


def _dot_nt(a, b):
    return lax.dot_general(a, b, (((1,), (1,)), ((), ())), preferred_element_type=_F32)


def _dot_tn(a, b):
    return lax.dot_general(a, b, (((0,), (0,)), ((), ())), preferred_element_type=_F32)


def _dot(a, b, precision=None):
    return jnp.dot(a, b, preferred_element_type=_F32, precision=precision)


HGRN_LEVELS = tuple(2 ** i for i in range(1, CHUNK.bit_length()))
LOG2E = 1.4426950408889634


def _hgrn_consts():
    C = CHUNK
    t = np.arange(C)[:, None]
    s = np.arange(C)[None, :]
    sgn = np.zeros((2, len(HGRN_LEVELS), C, LANES), np.float32)
    mask = np.zeros((2, len(HGRN_LEVELS) + 1, C, C), np.float32)
    tri = np.zeros((2, C, C), np.float32)
    for d in range(2):
        for li, blk in enumerate(HGRN_LEVELS):
            half = blk // 2
            q_t = ((t % blk) >= half) if d == 0 else ((t % blk) < half)
            q_s = ((s % blk) >= half) if d == 0 else ((s % blk) < half)
            sgn[d, li] = np.where(q_t, 1.0, -1.0)
            mask[d, li] = ((t // blk) == (s // blk)) & q_t & ~q_s
        mask[d, -1] = (t == s)
        tri[d] = (s <= t) if d == 0 else (s >= t)
    return jnp.asarray(tri, _BF), jnp.asarray(sgn), jnp.asarray(mask)


def _ref_rows(G, blk, reverse):
    C = G.shape[0]
    half = blk // 2
    rr = half if reverse else half - 1
    if blk >= SUBLANES:
        GB = G.reshape(C // blk, blk, LANES)
        return jnp.broadcast_to(GB[:, rr:rr + 1, :], GB.shape).reshape(C, LANES)
    if blk == 4:
        G3 = G.reshape(C // SUBLANES, SUBLANES, LANES)
        sub = lax.broadcasted_iota(jnp.int32, G3.shape, 1)
        lo = jnp.broadcast_to(G3[:, rr:rr + 1, :], G3.shape)
        hi = jnp.broadcast_to(G3[:, rr + 4:rr + 5, :], G3.shape)
        return jnp.where(sub < 4, lo, hi).reshape(C, LANES)
    odd = (lax.broadcasted_iota(jnp.int32, G.shape, 0) & 1) == 1
    if reverse:
        return jnp.where(odd, G, pltpu.roll(G, C - 1, 0))
    return jnp.where(odd, pltpu.roll(G, 1, 0), G)


def _hgrn_intra(qq, kk, vb, lf, d, tri_ref, sgn_ref, mask_ref):
    C = CHUNK
    l1 = lf.astype(_BF)
    r1 = lf - l1.astype(_F32)
    l2 = r1.astype(_BF)
    l3 = (r1 - l2.astype(_F32)).astype(_BF)
    g3 = _dot(tri_ref[d], jnp.concatenate([l1, l2, l3], axis=1))
    G2 = (g3[:, 0:LANES] + (g3[:, LANES:2 * LANES] + g3[:, 2 * LANES:3 * LANES])) * LOG2E
    scores = mask_ref[d, len(HGRN_LEVELS)] * _dot_nt(qq.astype(_BF), kk.astype(_BF))
    for li, blk in enumerate(HGRN_LEVELS):
        sg = sgn_ref[d, li]
        e = jnp.exp2((G2 - _ref_rows(G2, blk, d == 1)) * sg)
        x = (jnp.where(sg > 0.0, qq, kk) * e).astype(_BF)
        scores = scores + mask_ref[d, li] * _dot_nt(x, x)
    o = _dot(scores.astype(_BF), vb)
    qdec = (qq * jnp.exp2(G2)).astype(_BF)
    gend = G2[0:1, :] if d == 1 else G2[C - 1:C, :]
    kdec = (kk * jnp.exp2(gend - G2)).astype(_BF)
    return o, qdec, _dot_tn(vb, kdec), jnp.exp2(gend)


def _chunk_loop(n, body):
    if n <= 8:
        for c in range(n):
            body(c, 0)
    else:
        lax.fori_loop(0, n, body, 0, unroll=2)


def _hgrn2_kernel(*refs, L, has_state, emit_state, heads):
    for hh in range(heads):
        _hgrn2_head(*refs, L=L, has_state=has_state, emit_state=emit_state, hh=hh)


def _hgrn2_head(*refs, L, has_state, emit_state, hh):
    it = iter(refs)
    q_ref, v_ref, zf_ref, zb_ref, gz_ref, lb_ref, gn_ref, tri_ref, sgn_ref, mask_ref = (next(it) for _ in range(10))
    s0_ref = next(it) if has_state else None
    o_ref = next(it)
    sout_ref = next(it) if emit_state else None
    oacc, qdec_buf, ubuf, dbuf, st_ref = (next(it) for _ in range(5))
    nchunk = L // CHUNK
    ls = slice(hh * LANES, (hh + 1) * LANES)

    def rows_of(c):
        start = c * CHUNK
        return pl.ds(start if isinstance(c, int) else pl.multiple_of(start, CHUNK), CHUNK)

    def local(c, carry):
        rows = rows_of(c)
        qq = q_ref[rows, ls]
        qq = qq * jax.nn.sigmoid(qq)
        vb = v_ref[rows, ls].astype(_BF)
        for d, z_ref in enumerate((zf_ref, zb_ref)):
            z = z_ref[rows, ls]
            lb = lb_ref[d:d + 1, ls]
            lf = jnp.log(jnp.maximum(lb + (1.0 - lb) * jax.nn.sigmoid(z), GATE_FLOOR))
            kk = (1.0 - lb) * jax.nn.sigmoid(-z)
            o, qd, u, dec = _hgrn_intra(qq, kk, vb, lf, d, tri_ref, sgn_ref, mask_ref)
            oacc[d, rows, :] = o
            qdec_buf[d, rows, :] = qd
            ubuf[d, c] = u
            dbuf[d, c] = dec
        return carry

    _chunk_loop(nchunk, local)

    for d in range(2):
        st_ref[d] = s0_ref[d, hh].T if has_state else jnp.zeros((C_DV, C_DK), _F32)

    def scan(i, carry):
        for d in range(2):
            c = (nchunk - 1 - i) if d == 1 else i
            rows = rows_of(c)
            st = st_ref[d]
            oacc[d, rows, :] = oacc[d, rows, :] + _dot_nt(qdec_buf[d, rows, :], st.astype(_BF))
            st_ref[d] = st * dbuf[d, c] + ubuf[d, c]
        return carry

    _chunk_loop(nchunk, scan)
    if emit_state:
        for d in range(2):
            sout_ref[d, hh] = st_ref[d].T

    def finish(c, carry):
        rows = rows_of(c)
        tot = oacc[0, rows, :] + oacc[1, rows, :]
        g = gz_ref[rows, ls]
        o_ref[rows, ls] = (_rms_lanes(tot, gn_ref[...]) * (g * jax.nn.sigmoid(g))).astype(o_ref.dtype)
        return carry

    _chunk_loop(nchunk, finish)


MIX_COL = dict(a=0, b=GROUP_W, c=2 * GROUP_W, d=3 * GROUP_W)
MIX_SDS = jax.ShapeDtypeStruct((N_TOK, D_MODEL), _BF)


def _hgrn2(proj, lb, gnorm, s0, mix, states, *, row0, nb, L, layer):
    has_state = s0 is not None
    emit_state = not has_state
    rb0 = row0 // L
    nchunk = L // CHUNK
    tri, sgn, mask = _hgrn_consts()

    heads = max(1, DEC_SEQ // L)
    cw = heads * LANES

    def col(off):
        return pl.BlockSpec((L, cw), lambda b, h, off=off: (rb0 + b, off // cw + h))

    in_specs = [col(OFF_QC), col(OFF_IC), col(OFF_FF), col(OFF_FB), col(OFF_GZ),
                pl.BlockSpec((2, cw), lambda b, h: (0, h)),
                pl.BlockSpec((1, C_DV), lambda b, h: (0, 0)),
                _const_spec(tri.shape), _const_spec(sgn.shape), _const_spec(mask.shape)]
    args = [proj, proj, proj, proj, proj, lb, gnorm, tri, sgn, mask]
    st_spec = pl.BlockSpec((None, None, 2, heads, C_DK, C_DV), lambda b, h: (b, layer, 0, h, 0, 0))
    if has_state:
        in_specs.append(st_spec)
        args.append(s0)
    out_shape = [MIX_SDS]
    out_specs = [pl.BlockSpec((L, cw), lambda b, h: (rb0 + b, MIX_COL["c"] // cw + h))]
    bufs = [mix]
    if emit_state:
        out_shape.append(jax.ShapeDtypeStruct((nb, DEPTH, 2, C_HEADS, C_DK, C_DV), _F32))
        out_specs.append(st_spec)
        bufs.append(states)
    res = _call_into(
        functools.partial(_hgrn2_kernel, L=L, has_state=has_state, emit_state=emit_state, heads=heads), bufs,
        grid=(nb, C_HEADS // heads),
        in_specs=in_specs,
        args=args,
        out_specs=out_specs,
        out_shape=out_shape,
        scratch_shapes=[pltpu.VMEM((2, L, C_DV), _F32), pltpu.VMEM((2, L, C_DK), _BF),
                        pltpu.VMEM((2, nchunk, C_DV, C_DK), _F32), pltpu.VMEM((2, nchunk, 1, C_DK), _F32),
                        pltpu.VMEM((2, C_DV, C_DK), _F32)],
        sem=("parallel", "parallel"),
        name="hgrn2_lat" if has_state else "hgrn2_ctx",
    )
    return (res[0], res[1]) if emit_state else (res[0], states)


MOD_ROWS = 8
MOD_TN = 512


def _mod_row(i, tm):
    start = i * tm
    return jnp.where(start < N_CTX, 0, 1 + (start - N_CTX) // DEC_SEQ)


def _mod_kernel(c_ref, w_ref, b_ref, o_ref):
    cv = c_ref[...]
    s = (cv * jax.nn.sigmoid(cv)).astype(_BF)
    o_ref[...] = _dot(s, w_ref[...].astype(_BF)) + b_ref[...]


def _modulation(cvec, w_mod, b_mod):
    n = 6 * D_MODEL
    out = pl.pallas_call(
        _mod_kernel,
        grid=(DEPTH, n // MOD_TN),
        in_specs=[pl.BlockSpec((MOD_ROWS, D_MODEL), lambda l, j: (0, 0)),
                  pl.BlockSpec((None, D_MODEL, MOD_TN), lambda l, j: (l, 0, j)),
                  pl.BlockSpec((None, 1, MOD_TN), lambda l, j: (l, 0, j))],
        out_specs=pl.BlockSpec((None, MOD_ROWS, MOD_TN), lambda l, j: (l, 0, j)),
        out_shape=jax.ShapeDtypeStruct((DEPTH, MOD_ROWS, n), _F32),
        compiler_params=_cparams(("parallel", "parallel")),
        name="modulation",
    )(cvec, w_mod, b_mod.reshape(DEPTH, 1, n))
    return out.reshape(DEPTH, MOD_ROWS, 6, 1, D_MODEL).transpose(0, 2, 1, 3, 4)


NORM_TM = 512


def _norm_kernel(x_ref, g_ref, *rest, modulate):
    x = x_ref[...]
    y = x * lax.rsqrt(jnp.mean(x * x, axis=-1, keepdims=True) + EPS) * g_ref[...]
    if modulate:
        sc_ref, sh_ref, o_ref = rest
        y = y * (1.0 + sc_ref[...]) + sh_ref[...]
    else:
        (o_ref,) = rest
    o_ref[...] = y.astype(o_ref.dtype)


def _norm(x, g, mods=None, sc_idx=0, sh_idx=0, row0=0, nrows=N_TOK):
    modulate = mods is not None
    tm = NORM_TM
    t0 = row0 // tm
    in_specs = [pl.BlockSpec((tm, D_MODEL), lambda i: (t0 + i, 0)),
                pl.BlockSpec((1, D_MODEL), lambda i: (0, 0))]
    args = [x, g.reshape(1, D_MODEL)]
    if modulate:
        in_specs += [pl.BlockSpec((None, None, 1, D_MODEL), lambda i: (sc_idx, _mod_row(t0 + i, tm), 0, 0)),
                     pl.BlockSpec((None, None, 1, D_MODEL), lambda i: (sh_idx, _mod_row(t0 + i, tm), 0, 0))]
        args += [mods, mods]
    return pl.pallas_call(
        functools.partial(_norm_kernel, modulate=modulate),
        grid=(nrows // tm,),
        in_specs=in_specs,
        out_specs=pl.BlockSpec((tm, D_MODEL), lambda i: (i, 0)),
        out_shape=jax.ShapeDtypeStruct((nrows, D_MODEL), _BF if modulate else _F32),
        compiler_params=_cparams(("parallel",)),
        name="norm_mod" if modulate else "norm_final",
    )(*args)


def _mm_kernel(a_ref, b_ref, *rest, epilogue, nk, grid, side_chunks):
    ns = len(side_chunks)
    if epilogue == "resid":
        x_ref, gt_ref = rest[:2]
        rest = rest[2:]
    src_refs, o_ref, dst_refs, rest = rest[:ns], rest[ns], rest[ns + 1:2 * ns + 1], rest[2 * ns + 1:]

    step = (pl.program_id(0) * grid[1] + pl.program_id(1)) * grid[2] + pl.program_id(2)
    for src_ref, dst_ref, nchunks in zip(src_refs, dst_refs, side_chunks):
        @pl.when(step < nchunks)
        def _(src_ref=src_ref, dst_ref=dst_ref):
            dst_ref[...] = src_ref[...].astype(dst_ref.dtype)

    def finish(acc):
        if epilogue == "resid":
            o_ref[...] = x_ref[...] + gt_ref[...] * acc
        elif epilogue == "relu2":
            o_ref[...] = jnp.square(jnp.maximum(acc, 0.0)).astype(o_ref.dtype)
        else:
            o_ref[...] = acc.astype(o_ref.dtype)

    if nk == 1:
        finish(_dot(a_ref[...], b_ref[...]))
        return
    (acc_ref,) = rest
    k = pl.program_id(2)

    @pl.when(k == 0)
    def _():
        acc_ref[...] = jnp.zeros_like(acc_ref)

    acc_ref[...] += _dot(a_ref[...], b_ref[...])

    @pl.when(k == nk - 1)
    def _():
        finish(acc_ref[...])


MM_TILES = dict(in_proj=(1024, 1280, D_MODEL), out_proj=(1024, 1024, D_MODEL),
                mlp1=(1024, 1024, D_MODEL), mlp2=(1024, 1024, 2048))


SIDE_CHUNKS = 128


def _matmul(a, b, *, name, epilogue="plain", out_dtype=_F32, x=None, mods=None, gt_idx=0, side=()):
    M, K = a.shape
    N = b.shape[1]
    tm, tn, tk = MM_TILES[name]
    nk = K // tk
    grid = (M // tm, N // tn, nk)
    assert not side or grid[0] * grid[1] * grid[2] >= SIDE_CHUNKS
    in_specs = [pl.BlockSpec((tm, tk), lambda i, j, k: (i, k)),
                pl.BlockSpec((tk, tn), lambda i, j, k: (k, j))]
    args = [a, b]
    if epilogue == "resid":
        in_specs += [pl.BlockSpec((tm, tn), lambda i, j, k: (i, j)),
                     pl.BlockSpec((None, None, 1, tn), lambda i, j, k: (gt_idx, _mod_row(i, tm), 0, j))]
        args += [x, mods]

    def chunk(i, j, k):
        return jnp.minimum((i * grid[1] + j) * grid[2] + k, SIDE_CHUNKS - 1)

    out_specs = [pl.BlockSpec((tm, tn), lambda i, j, k: (i, j))]
    out_shape = [jax.ShapeDtypeStruct((M, N), out_dtype)]
    for w, layer in side:
        _, R, C = w.shape
        rows = R // SIDE_CHUNKS
        in_specs.append(pl.BlockSpec((None, rows, C), lambda i, j, k, layer=layer: (layer, chunk(i, j, k), 0)))
        args.append(w)
        out_specs.append(pl.BlockSpec((rows, C), lambda i, j, k: (chunk(i, j, k), 0)))
        out_shape.append(jax.ShapeDtypeStruct((R, C), _BF))
    return pl.pallas_call(
        functools.partial(_mm_kernel, epilogue=epilogue, nk=nk, grid=grid, side_chunks=(SIDE_CHUNKS,) * len(side)),
        grid=grid,
        in_specs=in_specs,
        out_specs=out_specs,
        out_shape=out_shape,
        scratch_shapes=[pltpu.VMEM((tm, tn), _F32)] if nk > 1 else [],
        compiler_params=_cparams(("arbitrary", "arbitrary", "arbitrary")),
        name=name,
    )(*args)


def _in_proj_kernel(a_ref, b_hbm, o_hbm, *, tm, tn, K, N):
    i = pl.program_id(0)

    def step(b_ref, o_ref):
        o_ref[...] = _dot(a_ref[...], b_ref[...])

    pltpu.emit_pipeline(
        step,
        grid=(N // tn,),
        in_specs=[pl.BlockSpec((K, tn), lambda j: (0, j))],
        out_specs=[pl.BlockSpec((tm, tn), lambda j: (i, j))],
    )(b_hbm, o_hbm)


def _in_proj(a, b):
    M, K = a.shape
    N = b.shape[1]
    tm, tn, _ = MM_TILES["in_proj"]
    return pl.pallas_call(
        functools.partial(_in_proj_kernel, tm=tm, tn=tn, K=K, N=N),
        grid=(M // tm,),
        in_specs=[pl.BlockSpec((tm, K), lambda i: (i, 0)), pl.BlockSpec(memory_space=pl.ANY)],
        out_specs=pl.BlockSpec(memory_space=pl.ANY),
        out_shape=jax.ShapeDtypeStruct((M, N), _F32),
        compiler_params=_cparams(("arbitrary",)),
        name="in_proj",
    )(a, b)


ATT_TQ = 256
ATT_SCALE = HEAD_DIM ** -0.5


def _rope(x, cos, sin):
    lane = lax.broadcasted_iota(jnp.int32, x.shape, 1)
    first = (lane % (2 * ROPE_FREQS)) < ROPE_FREQS
    partner = jnp.where(first, pltpu.roll(x, HEAD_DIM - ROPE_FREQS, 1), pltpu.roll(x, ROPE_FREQS, 1))
    return x * cos + partner * sin


def _rope_tables(L):
    pos = jnp.arange(L)
    rows = (pos // GRID_W).astype(_F32)
    cols = (pos % GRID_W).astype(_F32)
    inv = ROPE_BASE ** (-jnp.arange(ROPE_FREQS, dtype=_F32) / ROPE_FREQS)
    ang_r = rows[:, None] * inv
    ang_c = cols[:, None] * inv
    cos = jnp.concatenate([jnp.cos(ang_r)] * 2 + [jnp.cos(ang_c)] * 2, axis=-1)
    sin = jnp.concatenate([-jnp.sin(ang_r), jnp.sin(ang_r), -jnp.sin(ang_c), jnp.sin(ang_c)], axis=-1)
    return cos, sin


Q_PRESCALE = ATT_SCALE * LOG2E


def _softmax_parts(s2):
    e = jnp.exp2(s2 - jnp.max(s2, axis=-1, keepdims=True))
    return e, 1.0 / jnp.sum(e, axis=-1, keepdims=True)


def _rms_lanes(x, g):
    return x * lax.rsqrt(jnp.mean(x * x, axis=-1, keepdims=True) + EPS) * g


def _diff_kernel(*refs, L, latent, lam_init):
    it = iter(refs)
    q_ref, k_ref, v_ref = next(it), next(it), next(it)
    if latent:
        ck_ref, cv_ref, cos_ref, sin_ref = next(it), next(it), next(it), next(it)
    lam_ref, sub_ref, o_ref = next(it), next(it), next(it)
    if not latent:
        ak_ref, av_ref = next(it), next(it)
    kbuf, vbuf = next(it), next(it)
    w = 2 * HEAD_DIM

    lp = lam_ref[...]
    lam = (jnp.exp(jnp.sum(lp[0:1] * lp[1:2], axis=-1, keepdims=True))
           - jnp.exp(jnp.sum(lp[2:3] * lp[3:4], axis=-1, keepdims=True)) + lam_init)
    for h in range(1 if latent else A_HEADS):
        base = h * w
        for cpt in range(2):
            cs = slice(base + cpt * HEAD_DIM, base + (cpt + 1) * HEAD_DIM)
            kc = k_ref[:, cs]
            if latent:
                kc = _rope(kc, cos_ref[...], sin_ref[...])
                kbuf[cpt, L:, :] = ck_ref[:, cs].astype(_BF)
            kbuf[cpt, 0:L, :] = kc.astype(_BF)
        vv = v_ref[:, base:base + w]
        vbuf[0:L, :] = vv.astype(_BF)
        if latent:
            vbuf[L:, :] = cv_ref[...].astype(_BF)
        else:
            ak_ref[:, h, :] = k_ref[:, base:base + w]
            av_ref[:, h, :] = vv

        for qi in range(L // ATT_TQ):
            rows = slice(qi * ATT_TQ, (qi + 1) * ATT_TQ)
            parts = []
            for cpt in range(2):
                qc = q_ref[rows, base + cpt * HEAD_DIM:base + (cpt + 1) * HEAD_DIM]
                if latent:
                    qc = _rope(qc, cos_ref[rows, :], sin_ref[rows, :])
                parts.append(_softmax_parts(_dot_nt((qc * Q_PRESCALE).astype(_BF), kbuf[cpt])))
            (e0, r0), (e1, r1) = parts
            a = (e0 * r0 - e1 * (lam * r1)).astype(_BF)
            o = _dot(a, vbuf[...])
            o_ref[rows, base:base + w] = (_rms_lanes(o, sub_ref[...]) * (1.0 - lam_init)).astype(o_ref.dtype)


def _diff_attn(proj, lam_p, subln, lam_init, mix, new_k, new_v, *, row0, nb, L, layer,
               cache_k=None, cache_v=None, rope=None):
    latent = cache_k is not None
    rb0 = row0 // L
    w = 2 * HEAD_DIM
    nkeys = L + (PAST_LEN if latent else 0)
    cw = w if latent else GROUP_W

    def col(off):
        return pl.BlockSpec((L, cw), lambda b, h, off=off: (rb0 + b, off // cw + h))

    in_specs = [col(OFF_QA), col(OFF_KA), col(OFF_VA)]
    args = [proj, proj, proj]
    if latent:
        cspec = pl.BlockSpec((None, None, PAST_LEN, w), lambda b, h: (b, layer, 0, h))
        tspec = pl.BlockSpec((L, HEAD_DIM), lambda b, h: (0, 0))
        in_specs += [cspec, cspec, tspec, tspec]
        args += [cache_k.reshape(nb, DEPTH, PAST_LEN, GROUP_W), cache_v.reshape(nb, DEPTH, PAST_LEN, GROUP_W), rope[0], rope[1]]
    in_specs += [pl.BlockSpec((4, HEAD_DIM), lambda b, h: (0, 0)), pl.BlockSpec((1, w), lambda b, h: (0, 0))]
    args += [lam_p, subln.reshape(1, w)]
    out_shape = [MIX_SDS]
    out_specs = [pl.BlockSpec((L, cw), lambda b, h: (rb0 + b, MIX_COL["a"] // cw + h))]
    bufs = [mix]
    if not latent:
        kv_sds = jax.ShapeDtypeStruct((nb, DEPTH, L, A_HEADS, w), _F32)
        kv_spec = pl.BlockSpec((None, None, L, A_HEADS, w), lambda b, h: (b, layer, 0, 0, 0))
        out_shape += [kv_sds, kv_sds]
        out_specs += [kv_spec, kv_spec]
        bufs += [new_k, new_v]
    res = _call_into(
        functools.partial(_diff_kernel, L=L, latent=latent, lam_init=lam_init), bufs,
        grid=(nb, A_HEADS if latent else 1),
        in_specs=in_specs,
        args=args,
        out_specs=out_specs,
        out_shape=out_shape,
        scratch_shapes=[pltpu.VMEM((2, nkeys, HEAD_DIM), _BF), pltpu.VMEM((nkeys, w), _BF)],
        sem=("parallel", "parallel"),
        name="diff_attn_lat" if latent else "diff_attn_ctx",
    )
    return (res[0], new_k, new_v) if latent else tuple(res)


def _gqa_kernel(*refs, L, latent):
    it = iter(refs)
    q_ref, k_ref, v_ref = next(it), next(it), next(it)
    if latent:
        ck_ref, cv_ref, cos_ref, sin_ref = next(it), next(it), next(it), next(it)
    qn_ref, kn_ref, o_ref = next(it), next(it), next(it)
    if not latent:
        dk_ref, dv_ref = next(it), next(it)
    kbuf, vbuf = next(it), next(it)
    wq = D_GROUPS * HEAD_DIM

    for n in range(1 if latent else D_KV_HEADS):
        ks = slice(n * HEAD_DIM, (n + 1) * HEAD_DIM)
        kn = _rms_lanes(k_ref[:, ks], kn_ref[...])
        vv = v_ref[:, ks]
        if latent:
            kn = _rope(kn, cos_ref[...], sin_ref[...])
            kbuf[L:, :] = ck_ref[...].astype(_BF)
            vbuf[L:, :] = cv_ref[...].astype(_BF)
        else:
            dk_ref[:, n, :] = kn
            dv_ref[:, n, :] = vv
        kbuf[0:L, :] = kn.astype(_BF)
        vbuf[0:L, :] = vv.astype(_BF)

        for g in range(D_GROUPS):
            cs = slice(n * wq + g * HEAD_DIM, n * wq + (g + 1) * HEAD_DIM)
            for qi in range(L // ATT_TQ):
                rows = slice(qi * ATT_TQ, (qi + 1) * ATT_TQ)
                qg = _rms_lanes(q_ref[rows, cs], qn_ref[...])
                if latent:
                    qg = _rope(qg, cos_ref[rows, :], sin_ref[rows, :])
                e, r = _softmax_parts(_dot_nt((qg * Q_PRESCALE).astype(_BF), kbuf[...]))
                o_ref[rows, cs] = (_dot(e.astype(_BF), vbuf[...]) * r).astype(o_ref.dtype)


def _gqa(proj, qnorm, knorm, mix, new_k, new_v, *, row0, nb, L, layer, cache_k=None, cache_v=None, rope=None):
    latent = cache_k is not None
    rb0 = row0 // L
    nh = 1 if latent else D_KV_HEADS
    wq = nh * D_GROUPS * HEAD_DIM
    wk = nh * HEAD_DIM
    nkeys = L + (PAST_LEN if latent else 0)
    in_specs = [pl.BlockSpec((L, wq), lambda b, n: (rb0 + b, OFF_QD // wq + n)),
                pl.BlockSpec((L, wk), lambda b, n: (rb0 + b, OFF_KD // wk + n)),
                pl.BlockSpec((L, wk), lambda b, n: (rb0 + b, OFF_VD // wk + n))]
    args = [proj, proj, proj]
    if latent:
        cspec = pl.BlockSpec((None, None, PAST_LEN, HEAD_DIM), lambda b, n: (b, layer, 0, n))
        tspec = pl.BlockSpec((L, HEAD_DIM), lambda b, n: (0, 0))
        in_specs += [cspec, cspec, tspec, tspec]
        kvw = D_KV_HEADS * HEAD_DIM
        args += [cache_k.reshape(nb, DEPTH, PAST_LEN, kvw), cache_v.reshape(nb, DEPTH, PAST_LEN, kvw), rope[0], rope[1]]
    gspec = pl.BlockSpec((1, HEAD_DIM), lambda b, n: (0, 0))
    in_specs += [gspec, gspec]
    args += [qnorm.reshape(1, HEAD_DIM), knorm.reshape(1, HEAD_DIM)]
    out_shape = [MIX_SDS]
    out_specs = [pl.BlockSpec((L, wq), lambda b, n: (rb0 + b, MIX_COL["d"] // wq + n))]
    bufs = [mix]
    if not latent:
        kv_sds = jax.ShapeDtypeStruct((nb, DEPTH, L, D_KV_HEADS, HEAD_DIM), _F32)
        kv_spec = pl.BlockSpec((None, None, L, D_KV_HEADS, HEAD_DIM), lambda b, n: (b, layer, 0, 0, 0))
        out_shape += [kv_sds, kv_sds]
        out_specs += [kv_spec, kv_spec]
        bufs += [new_k, new_v]
    res = _call_into(
        functools.partial(_gqa_kernel, L=L, latent=latent), bufs,
        grid=(nb, D_KV_HEADS // nh),
        in_specs=in_specs,
        args=args,
        out_specs=out_specs,
        out_shape=out_shape,
        scratch_shapes=[pltpu.VMEM((nkeys, HEAD_DIM), _BF), pltpu.VMEM((nkeys, HEAD_DIM), _BF)],
        sem=("parallel", "parallel"),
        name="gqa_lat" if latent else "gqa_ctx",
    )
    return (res[0], new_k, new_v) if latent else tuple(res)


HY_CB = 256


def _dft_mats(L):
    f = np.arange(L, dtype=np.float64)[:, None] + 0.5
    t = np.arange(L, dtype=np.float64)[None, :]
    ang = 2.0 * np.pi * f * t / (2 * L)
    out = []
    for m in (np.cos(ang), np.sin(ang), np.cos(ang).T, np.sin(ang).T):
        m32 = jnp.asarray(m.astype(np.float32))
        hi = m32.astype(_BF)
        out += [hi, (m32 - hi.astype(_F32)).astype(_BF)]
    return tuple(out)


def _split_bf16(x):
    hi = x.astype(_BF)
    return hi, (x - hi.astype(_F32)).astype(_BF)


def _dot3(m_hi, m_lo, x):
    xh, xl = _split_bf16(x)
    return _dot(m_hi, xh) + (_dot(m_hi, xl) + _dot(m_lo, xh))


def _filter_features(L):
    t = jnp.arange(L, dtype=_F32) / L
    bands = jnp.arange(1, HY_NFREQ + 1, dtype=_F32)
    ang = 2.0 * math.pi * t[:, None] * bands
    z = jnp.concatenate([t[:, None], jnp.cos(ang), jnp.sin(ang)], axis=-1)
    return z, t[:, None]


def _filt_kernel(z_ref, t_ref, w1_ref, b1_ref, w2_ref, b2_ref, w3f_ref, w3b_ref, fr_ref, df_ref, db_ref, sk_ref,
                 cmh_ref, cml_ref, smh_ref, sml_ref, hr_ref, hi_ref, *, L):
    fr = fr_ref[...]
    hid = jnp.sin(fr * (_dot(z_ref[...], w1_ref[...], _HI) + b1_ref[...]))
    hid = jnp.sin(fr * (_dot(hid, w2_ref[...], _HI) + b2_ref[...]))
    t = t_ref[...]

    def filt(w3_ref, d_ref):
        h = _dot(hid, w3_ref[...], _HI) * (jnp.exp(-t * jnp.abs(d_ref[...])) + HY_SHIFT)
        return h / (jnp.sum(jnp.abs(h), axis=0, keepdims=True) + EPS)

    hf = filt(w3f_ref, df_ref)
    hb = filt(w3b_ref, db_ref)
    hr_ref[...] = (_dot3(cmh_ref[...], cml_ref[...], hf + hb) + sk_ref[...]) * (1.0 / L)
    hi_ref[...] = _dot3(smh_ref[...], sml_ref[...], hb - hf) * (1.0 / L)


def _const_spec(shape):
    zeros = (0,) * len(shape)
    return pl.BlockSpec(shape, lambda *_: zeros, pipeline_mode=pl.Buffered(1))


def _hyena_filters(L, mats, w1, b1, w2, b2, w3, freq, delta, skip):
    z, t = _filter_features(L)
    cb = 2 * HY_CB
    ncb = B_W // cb
    full = lambda shape: pl.BlockSpec(shape, lambda l, o, j: (0,) * len(shape))
    per_layer = lambda shape: pl.BlockSpec((None,) + shape, lambda l, o, j: (l,) + (0,) * len(shape))
    mspec = _const_spec((L, L))
    in_specs = [full((L, HY_EMB)), full((L, 1)),
                per_layer((HY_EMB, HY_FFN)), per_layer((1, HY_FFN)),
                per_layer((HY_FFN, HY_FFN)), per_layer((1, HY_FFN)),
                pl.BlockSpec((None, HY_FFN, cb), lambda l, o, j: (l, 0, o * ncb + j)),
                pl.BlockSpec((None, HY_FFN, cb), lambda l, o, j: (l, 0, (2 + o) * ncb + j)),
                per_layer((1, HY_FFN)),
                pl.BlockSpec((None, None, 1, cb), lambda l, o, j: (l, o, 0, j)),
                pl.BlockSpec((None, None, 1, cb), lambda l, o, j: (l, 2 + o, 0, j)),
                pl.BlockSpec((None, None, 1, cb), lambda l, o, j: (l, o, 0, j)),
                mspec, mspec, mspec, mspec]
    out_spec = pl.BlockSpec((None, None, L, cb), lambda l, o, j: (l, o, 0, j))
    out_sds = jax.ShapeDtypeStruct((DEPTH, 2, L, B_W), _F32)
    return pl.pallas_call(
        functools.partial(_filt_kernel, L=L),
        grid=(DEPTH, 2, ncb),
        in_specs=in_specs,
        out_specs=[out_spec, out_spec],
        out_shape=[out_sds, out_sds],
        compiler_params=_cparams(("parallel", "parallel", "parallel")),
        name=f"hyena_filters_{L}",
    )(z, t, w1, b1.reshape(DEPTH, 1, HY_FFN), w2, b2.reshape(DEPTH, 1, HY_FFN), w3, w3,
      freq.reshape(DEPTH, 1, HY_FFN), delta.reshape(DEPTH, 4, 1, B_W), delta.reshape(DEPTH, 4, 1, B_W),
      skip.reshape(DEPTH, 2, 1, B_W), *mats[:4])


def _short_conv(u, w, b):
    L = u.shape[0]
    row = lax.broadcasted_iota(jnp.int32, u.shape, 0)
    prev = jnp.where(row == 0, 0.0, pltpu.roll(u, 1, 0))
    nxt = jnp.where(row == L - 1, 0.0, pltpu.roll(u, L - 1, 0))
    return prev * w[0:1, :] + u * w[1:2, :] + nxt * w[2:3, :] + b


def _hyena_kernel(v_ref, x1_ref, x2_ref, wv_ref, w1_ref, w2_ref, bv_ref, b1_ref, b2_ref, hr_ref, hi_ref,
                  cm_ref, sm_ref, cmt_ref, smt_ref, o_ref):
    def conv(u, o):
        ub = u.astype(_BF)
        a = _dot(cm_ref[...], ub)
        b = _dot(sm_ref[...], ub)
        hr = hr_ref[o]
        hi = hi_ref[o]
        return (_dot(cmt_ref[...], (a * hr + b * hi).astype(_BF))
                - _dot(smt_ref[...], (a * hi - b * hr).astype(_BF)))

    v = _short_conv(v_ref[...], wv_ref[...], bv_ref[...])
    z = _short_conv(x1_ref[...], w1_ref[...], b1_ref[...]) * conv(v, 0)
    y = _short_conv(x2_ref[...], w2_ref[...], b2_ref[...]) * conv(z, 1)
    o_ref[...] = y.astype(o_ref.dtype)


def _hyena(proj, conv_w, conv_b, hr, hi, mats, mix, *, row0, nb, L, layer):
    rb0 = row0 // L
    cb = min(B_W, HY_CB * DEC_SEQ // L)
    ncb = B_W // cb

    def ucol(part):
        return pl.BlockSpec((L, cb), lambda b, j, part=part: (rb0 + b, OFF_UB // cb + part * ncb + j))

    def wcol(rows, part):
        return pl.BlockSpec((rows, cb), lambda b, j, part=part: (0, part * ncb + j))

    hspec = pl.BlockSpec((None, 2, L, cb), lambda b, j: (layer, 0, 0, j))
    mspec = _const_spec((L, L))
    return _call_into(
        _hyena_kernel, [mix],
        grid=(nb, ncb),
        in_specs=[ucol(0), ucol(1), ucol(2), wcol(3, 0), wcol(3, 1), wcol(3, 2), wcol(1, 0), wcol(1, 1), wcol(1, 2),
                  hspec, hspec] + [mspec] * 4,
        args=[proj, proj, proj, conv_w, conv_w, conv_w, conv_b, conv_b, conv_b, hr, hi, *mats[0::2]],
        out_specs=[pl.BlockSpec((L, cb), lambda b, j: (rb0 + b, MIX_COL["b"] // cb + j))],
        out_shape=[MIX_SDS],
        scratch_shapes=[],
        sem=("parallel", "parallel"),
        name=f"hyena_{L}",
    )[0]


def _mixers(proj, l, lam_init, lp, mix, outs, *, row0, nb, L, caches, rope, filt, mats):
    kw = dict(row0=row0, nb=nb, L=L, layer=l)
    ak, av, dk, dv, st = outs
    if caches is not None:
        att = dict(rope=rope, **kw)
        mix, _, _ = _diff_attn(proj, lp["a_lam"], lp["a_subln"], lam_init, mix, ak, av,
                               cache_k=caches[0], cache_v=caches[1], **att)
        mix, _, _ = _gqa(proj, lp["d_qnorm"], lp["d_knorm"], mix, dk, dv, cache_k=caches[2], cache_v=caches[3], **att)
        mix, _ = _hgrn2(proj, lp["c_lb"], lp["c_gnorm"], caches[4], mix, st, **kw)
    else:
        mix, ak, av = _diff_attn(proj, lp["a_lam"], lp["a_subln"], lam_init, mix, ak, av, **kw)
        mix, dk, dv = _gqa(proj, lp["d_qnorm"], lp["d_knorm"], mix, dk, dv, **kw)
        mix, st = _hgrn2(proj, lp["c_lb"], lp["c_gnorm"], None, mix, st, **kw)
    mix = _hyena(proj, lp["b_conv_w"], lp["b_conv_b"], filt[0], filt[1], mats, mix, **kw)
    return mix, (ak, av, dk, dv, st)


def kernel(x_prompt, x_sample, c, cache_a_k, cache_a_v, cache_d_k, cache_d_v, state_c, c_ctx, w_mod, b_mod, g_norm1, g_norm2, w_in, w_out, a_lam, a_subln, b_conv_w, b_conv_b, b_ffn_w1, b_ffn_b1, b_ffn_w2, b_ffn_b2, b_ffn_w3, b_freq, b_delta, b_skip, c_lb_raw, c_gnorm, d_qnorm, d_knorm, w_mlp1, w_mlp2, g_final):
    p = jax.nn.softmax(c_lb_raw.astype(_F32), axis=0)
    lb_all = jnp.cumsum(p, axis=0) - p[:1]

    cvec = jnp.concatenate([c_ctx[None, :], c, jnp.zeros((MOD_ROWS - 1 - DEC_BATCH, D_MODEL), _F32)], axis=0)
    mods = _modulation(cvec, w_mod, b_mod)

    w_in_b, w_out_b, w_mlp1_b = w_in[0].astype(_BF), w_out[0].astype(_BF), w_mlp1[0].astype(_BF)

    mats_c, mats_l = _dft_mats(SEQ), _dft_mats(DEC_SEQ)
    fargs = (b_ffn_w1, b_ffn_b1, b_ffn_w2, b_ffn_b2, b_ffn_w3, b_freq, b_delta, b_skip)
    filt_c = _hyena_filters(SEQ, mats_c, *fargs)
    filt_l = _hyena_filters(DEC_SEQ, mats_l, *fargs)
    rope = _rope_tables(DEC_SEQ)

    x = jnp.concatenate([x_prompt.reshape(N_CTX, D_MODEL), x_sample.reshape(N_LAT, D_MODEL)], axis=0)
    outs = (None,) * 5
    for l in range(DEPTH):
        lam_init = 0.8 - 0.6 * math.exp(-0.3 * l)
        lp = dict(a_lam=a_lam[l], a_subln=a_subln[l], b_conv_w=b_conv_w[l], b_conv_b=b_conv_b[l].reshape(1, 3 * B_W),
                  c_lb=lb_all[l], c_gnorm=c_gnorm[l].reshape(1, C_DV), d_qnorm=d_qnorm[l], d_knorm=d_knorm[l])
        h = _norm(x, g_norm1[l], mods[l], sc_idx=1, sh_idx=0)
        proj = _in_proj(h, w_in_b)
        mix, outs = _mixers(proj, l, lam_init, lp, None, outs, row0=0, nb=BATCH, L=SEQ, caches=None, rope=None,
                            filt=filt_c, mats=mats_c)
        mix, outs = _mixers(proj, l, lam_init, lp, mix, outs, row0=N_CTX, nb=DEC_BATCH, L=DEC_SEQ,
                            caches=(cache_a_k, cache_a_v, cache_d_k, cache_d_v, state_c), rope=rope,
                            filt=filt_l, mats=mats_l)
        (x,) = _matmul(mix, w_out_b, name="out_proj", epilogue="resid", x=x, mods=mods[l], gt_idx=2)
        h2 = _norm(x, g_norm2[l], mods[l], sc_idx=4, sh_idx=3)
        more = l + 1 < DEPTH
        hid, w_mlp2_b, *cast = _matmul(h2, w_mlp1_b, name="mlp1", epilogue="relu2", out_dtype=_BF,
                                       side=[(w_mlp2, l)] + ([(w_in, l + 1)] if more else []))
        if more:
            (w_in_b,) = cast
        x, *cast = _matmul(hid, w_mlp2_b, name="mlp2", epilogue="resid", x=x, mods=mods[l], gt_idx=5,
                           side=[(w_mlp1, l + 1), (w_out, l + 1)] if more else [])
        if more:
            w_mlp1_b, w_out_b = cast

    y_prompt = _norm(x, g_final, row0=0, nrows=N_CTX).reshape(BATCH, SEQ, D_MODEL)
    y_sample = _norm(x, g_final, row0=N_CTX, nrows=N_LAT).reshape(DEC_BATCH, DEC_SEQ, D_MODEL)
    return (y_prompt, y_sample) + outs
```
